```python
import math
import jax
import jax.numpy as jnp
from jax import lax
import numpy as np

D_MODEL = 4096
BATCH = 2
SEQ = 4096
DEPTH = 2

GRID_W = 64
CTX_LEN = 256
HEAD_DIM = 128
MIX_WIDTH = D_MODEL
N_HEADS_TOTAL = MIX_WIDTH // HEAD_DIM
H_A = N_HEADS_TOTAL // 4
KV_A = H_A // 4
H_B = N_HEADS_TOTAL // 4
DH_B = HEAD_DIM // 2
H_C = N_HEADS_TOTAL // 4
KV_C = H_C // 4
H_D = N_HEADS_TOTAL // 4
WIN_C = 128
NA_KH = 8
NA_KW = 16
Q_BLOCK = 128
D_FF = ((8 * D_MODEL // 3 + 255) // 256) * 256
CONV_W = 3
ROPE_THETA = 10000.0
EPS = 1e-6
NEG_INF = -1e30

Q_SHAPES = ((H_A, HEAD_DIM), (H_B, 2, DH_B), (H_C, HEAD_DIM), (H_D, HEAD_DIM))
KV_SHAPES = ((KV_A, HEAD_DIM), (KV_A, HEAD_DIM), (H_B, 2, DH_B), (H_B, HEAD_DIM),
             (KV_C, HEAD_DIM), (KV_C, HEAD_DIM), (H_D, HEAD_DIM), (H_D, HEAD_DIM))
N_Q = sum(int(np.prod(s)) for s in Q_SHAPES)
N_KV = sum(int(np.prod(s)) for s in KV_SHAPES)
N_IN = N_Q + N_KV

kernel_name = 'hymba_style_four_group_flow_backbone'


def rmsnorm(x, g):
    xf = x.astype(jnp.float32)
    y = xf * lax.rsqrt(jnp.mean(xf * xf, axis=-1, keepdims=True) + EPS)
    return (y * g.astype(jnp.float32)).astype(x.dtype)


def axial_rope(head_dim, rows, cols):
    quarter = head_dim // 4
    inv = ROPE_THETA ** (-jnp.arange(quarter, dtype=jnp.float32) / quarter)
    ang = jnp.concatenate([rows[:, None] * inv, cols[:, None] * inv], axis=-1)
    return jnp.cos(ang), jnp.sin(ang)


def apply_rope(x, rope):
    cos = rope[0][None, :, None, :].astype(x.dtype)
    sin = rope[1][None, :, None, :].astype(x.dtype)
    x1, x2 = x[..., 0::2], x[..., 1::2]
    return jnp.stack([x1 * cos - x2 * sin, x1 * sin + x2 * cos], axis=-1).reshape(x.shape)


def project_heads(p, with_q):
    shapes = (Q_SHAPES if with_q else ()) + KV_SHAPES
    sizes = [int(np.prod(s)) for s in shapes]
    parts = jnp.split(p, np.cumsum(sizes)[:-1].tolist(), axis=-1)
    out = [t.reshape(t.shape[:2] + s) for t, s in zip(parts, shapes)]
    if not with_q:
        out = [None, None, None, None] + out
    return out


def gqa_attend(q, k, v, mask=None, sink=None):
    s = jnp.einsum('bqkgd,bskd->bkgqs', q, k).astype(jnp.float32) * (q.shape[-1] ** -0.5)
    if mask is not None:
        s = jnp.where(mask, s, NEG_INF)
    if sink is not None:
        sk = jnp.broadcast_to(sink.astype(jnp.float32)[None, :, :, None, None], s.shape[:-1] + (1,))
        p = jax.nn.softmax(jnp.concatenate([s, sk], axis=-1), axis=-1)[..., :-1]
    else:
        p = jax.nn.softmax(s, axis=-1)
    return jnp.einsum('bkgqs,bskd->bqkgd', p.astype(v.dtype), v)


def mixer_a(q, k, v, qc, kc, vc, g_q, g_k, rope):
    B, S, H, d = q.shape
    G = H // KV_A
    q = apply_rope(rmsnorm(q, g_q), rope).reshape(B, S, KV_A, G, d)
    k = apply_rope(rmsnorm(k, g_k), rope)
    kc = rmsnorm(kc, g_k)
    k_all = jnp.concatenate([k, kc], axis=1)
    v_all = jnp.concatenate([v, vc], axis=1)

    def block(i):
        qi = lax.dynamic_slice_in_dim(q, i * Q_BLOCK, Q_BLOCK, axis=1)
        return gqa_attend(qi, k_all, v_all)

    o = lax.map(block, jnp.arange(S // Q_BLOCK))
    o = jnp.moveaxis(o, 0, 1).reshape(B, S, H * d)
    oc = None
    if qc is not None:
        C = qc.shape[1]
        qcn = rmsnorm(qc, g_q).reshape(B, C, KV_A, G, d)
        oc = gqa_attend(qcn, kc, vc).reshape(B, C, H * d)
    return o, oc


def diff_attend(q, k, v, lam):
    s = jnp.einsum('bqhcd,bkhcd->bhcqk', q, k).astype(jnp.float32) * (q.shape[-1] ** -0.5)
    p = jax.nn.softmax(s, axis=-1)
    w = p[:, :, 0] - lam * p[:, :, 1]
    return jnp.einsum('bhqk,bkhe->bqhe', w.astype(v.dtype), v)


def mixer_b(q, k, v, qc, kc, vc, lam_q1, lam_k1, lam_q2, lam_k2, g_sub, lam_init, rope):
    B, S, H, _, dh = q.shape
    q = apply_rope(q.reshape(B, S, H * 2, dh), rope).reshape(B, S, H, 2, dh)
    k = apply_rope(k.reshape(B, S, H * 2, dh), rope).reshape(B, S, H, 2, dh)
    f32 = jnp.float32
    lam = (jnp.exp(jnp.sum(lam_q1.astype(f32) * lam_k1.astype(f32)))
           - jnp.exp(jnp.sum(lam_q2.astype(f32) * lam_k2.astype(f32))) + lam_init)
    k_all = jnp.concatenate([k, kc], axis=1)
    v_all = jnp.concatenate([v, vc], axis=1)

    def block(i):
        qi = lax.dynamic_slice_in_dim(q, i * Q_BLOCK, Q_BLOCK, axis=1)
        return diff_attend(qi, k_all, v_all, lam)

    o = lax.map(block, jnp.arange(S // Q_BLOCK))
    o = jnp.moveaxis(o, 0, 1).reshape(B, S, H, v.shape[-1])
    o = (rmsnorm(o, g_sub) * (1.0 - lam_init)).reshape(B, S, -1)
    oc = None
    if qc is not None:
        C = qc.shape[1]
        oc = (rmsnorm(diff_attend(qc, kc, vc, lam), g_sub) * (1.0 - lam_init)).reshape(B, C, -1)
    return o, oc


def mixer_c(q, k, v, qc, kc, vc, sink, rope):
    B, S, H, d = q.shape
    G = H // KV_C
    q = apply_rope(q, rope).reshape(B, S, KV_C, G, d)
    k = apply_rope(k, rope)
    pad = ((0, 0), (WIN_C, WIN_C), (0, 0), (0, 0))
    kp = jnp.pad(k, pad)
    vp = jnp.pad(v, pad)
    band = Q_BLOCK + 2 * WIN_C
    sink_g = sink.reshape(KV_C, G)
    ctx_ok = jnp.ones((Q_BLOCK, kc.shape[1]), dtype=bool)

    def block(i):
        s0 = i * Q_BLOCK
        qi = lax.dynamic_slice_in_dim(q, s0, Q_BLOCK, axis=1)
        kb = lax.dynamic_slice_in_dim(kp, s0, band, axis=1)
        vb = lax.dynamic_slice_in_dim(vp, s0, band, axis=1)
        qpos = s0 + jnp.arange(Q_BLOCK)
        kpos = s0 - WIN_C + jnp.arange(band)
        local = ((kpos >= 0) & (kpos < S))[None, :] & (jnp.abs(qpos[:, None] - kpos[None, :]) <= WIN_C)
        mask = jnp.concatenate([local, ctx_ok], axis=1)
        return gqa_attend(qi, jnp.concatenate([kb, kc], axis=1), jnp.concatenate([vb, vc], axis=1), mask, sink_g)

    o = lax.map(block, jnp.arange(S // Q_BLOCK))
    o = jnp.moveaxis(o, 0, 1).reshape(B, S, H * d)
    oc = None
    if qc is not None:
        C = qc.shape[1]
        oc = gqa_attend(qc.reshape(B, C, KV_C, G, d), kc, vc, sink=sink_g).reshape(B, C, H * d)
    return o, oc


def mixer_d(q, k, v, qc, kc, vc, bias_table, n_rows):
    B, S, H, d = q.shape
    kh = min(NA_KH, n_rows)
    kw = NA_KW
    scale = d ** -0.5
    qg = q.reshape(B, n_rows, GRID_W, H, d)
    kg = k.reshape(B, n_rows, GRID_W, H, d)
    vg = v.reshape(B, n_rows, GRID_W, H, d)
    cols = np.arange(GRID_W)
    col_idx = np.clip(cols - kw // 2, 0, GRID_W - kw)[:, None] + np.arange(kw)[None, :]
    col_off = col_idx - cols[:, None] + NA_KW - 1
    bias_cols = bias_table[:, :, col_off]

    def row(r):
        rs = jnp.clip(r - kh // 2, 0, n_rows - kh)
        kn = lax.dynamic_slice_in_dim(kg, rs, kh, axis=1)[:, :, col_idx]
        vn = lax.dynamic_slice_in_dim(vg, rs, kh, axis=1)[:, :, col_idx]
        qr = lax.dynamic_index_in_dim(qg, r, axis=1, keepdims=False)
        s_loc = jnp.einsum('bwhd,brwkhd->bhwrk', qr, kn).astype(jnp.float32) * scale
        roff = rs + jnp.arange(kh) - r + NA_KH - 1
        bias = jnp.take(bias_cols, roff, axis=1).transpose(0, 2, 1, 3)
        s_loc = (s_loc + bias[None].astype(jnp.float32)).reshape(B, H, GRID_W, kh * kw)
        s_ctx = jnp.einsum('bwhd,bchd->bhwc', qr, kc).astype(jnp.float32) * scale
        p = jax.nn.softmax(jnp.concatenate([s_loc, s_ctx], axis=-1), axis=-1).astype(v.dtype)
        p_loc = p[..., :kh * kw].reshape(B, H, GRID_W, kh, kw)
        p_ctx = p[..., kh * kw:]
        return (jnp.einsum('bhwrk,brwkhd->bwhd', p_loc, vn)
                + jnp.einsum('bhwc,bchd->bwhd', p_ctx, vc))

    o = lax.map(row, jnp.arange(n_rows))
    o = jnp.moveaxis(o, 0, 1).reshape(B, S, H * d)
    oc = None
    if qc is not None:
        C = qc.shape[1]
        oc = gqa_attend(qc[:, :, :, None, :], kc, vc).reshape(B, C, H * d)
    return o, oc


def conv_ffn(x, w_up, conv_w, w_down):
    L = x.shape[1]
    u = x @ w_up
    up = jnp.pad(u, ((0, 0), (CONV_W // 2, CONV_W // 2), (0, 0)))
    u = sum(up[:, j:j + L] * conv_w[j] for j in range(CONV_W))
    gate, val = jnp.split(u, 2, axis=-1)
    return (jax.nn.silu(gate) * val) @ w_down


def trunk_layer(x, h, silu_c, silu_cc, rope_hd, rope_dh, n_rows, layer_idx, last,
                w_ada, b_ada, g_attn_pre, g_attn_post, g_mlp_pre, g_mlp_post,
                w_in, w_out, g_q_a, g_k_a, lam_q1, lam_k1, lam_q2, lam_k2, g_sub_b,
                sink_c, na_bias, w_up, conv_w, w_down):
    D = x.shape[-1]
    mod = (silu_c @ w_ada + b_ada)[:, None, :]
    sh_a, sc_a, gt_a, sh_m, sc_m, gt_m = jnp.split(mod, 6, axis=-1)
    n_c = 2 if last else 6
    mod_c = jnp.split(silu_cc @ w_ada[:, :n_c * D] + b_ada[:n_c * D], n_c)
    hx = rmsnorm(x, g_attn_pre) * (1 + sc_a) + sh_a
    hc = rmsnorm(h, g_attn_pre) * (1 + mod_c[1]) + mod_c[0]
    lat = project_heads(hx @ w_in, True)
    cx = project_heads(hc @ (w_in[:, N_Q:] if last else w_in), not last)
    lam_init = 0.8 - 0.6 * math.exp(-0.3 * layer_idx)
    o_a, co_a = mixer_a(lat[0], lat[4], lat[5], cx[0], cx[4], cx[5], g_q_a, g_k_a, rope_hd)
    o_b, co_b = mixer_b(lat[1], lat[6], lat[7], cx[1], cx[6], cx[7],
                        lam_q1, lam_k1, lam_q2, lam_k2, g_sub_b, lam_init, rope_dh)
    o_c, co_c = mixer_c(lat[2], lat[8], lat[9], cx[2], cx[8], cx[9], sink_c, rope_hd)
    o_d, co_d = mixer_d(lat[3], lat[10], lat[11], cx[3], cx[10], cx[11], na_bias, n_rows)
    o = jnp.concatenate([o_a, o_b, o_c, o_d], axis=-1) @ w_out
    x = x + gt_a * rmsnorm(o, g_attn_post)
    hx = rmsnorm(x, g_mlp_pre) * (1 + sc_m) + sh_m
    x = x + gt_m * rmsnorm(conv_ffn(hx, w_up, conv_w, w_down), g_mlp_post)
    if not last:
        co = jnp.concatenate([co_a, co_b, co_c, co_d], axis=-1) @ w_out
        h = h + mod_c[2] * rmsnorm(co, g_attn_post)
        hc = rmsnorm(h, g_mlp_pre) * (1 + mod_c[4]) + mod_c[3]
        h = h + mod_c[5] * rmsnorm(conv_ffn(hc, w_up, conv_w, w_down), g_mlp_post)
    return x, h


def setup_inputs(seed: int = 0) -> dict:
    key = jax.random.key(seed)
    ks = jax.random.split(key, 24)
    f32 = jnp.float32
    L, D = DEPTH, D_MODEL

    def nrm(k, shape, scale):
        return jax.random.normal(k, shape, f32) * scale

    return {
        'x': nrm(ks[0], (BATCH, SEQ, D), 1.0),
        'c': nrm(ks[1], (BATCH, D), 1.0),
        'ctx': nrm(ks[2], (BATCH, CTX_LEN, D), 1.0),
        'c_ctx': nrm(ks[3], (D,), 1.0),
        'w_ada': nrm(ks[4], (L, D, 6 * D), 0.5 * D ** -0.5),
        'b_ada': nrm(ks[5], (L, 6 * D), 0.01),
        'g_attn_pre': 1.0 + nrm(ks[6], (L, D), 0.02),
        'g_attn_post': 1.0 + nrm(ks[7], (L, D), 0.02),
        'g_mlp_pre': 1.0 + nrm(ks[8], (L, D), 0.02),
        'g_mlp_post': 1.0 + nrm(ks[9], (L, D), 0.02),
        'w_in': nrm(ks[10], (L, D, N_IN), D ** -0.5),
        'w_out': nrm(ks[11], (L, MIX_WIDTH, D), MIX_WIDTH ** -0.5),
        'g_q_a': 1.0 + nrm(ks[12], (L, HEAD_DIM), 0.02),
        'g_k_a': 1.0 + nrm(ks[13], (L, HEAD_DIM), 0.02),
        'lam_q1': nrm(ks[14], (L, DH_B), 0.1),
        'lam_k1': nrm(ks[15], (L, DH_B), 0.1),
        'lam_q2': nrm(ks[16], (L, DH_B), 0.1),
        'lam_k2': nrm(ks[17], (L, DH_B), 0.1),
        'g_sub_b': 1.0 + nrm(ks[18], (L, HEAD_DIM), 0.02),
        'sink_c': nrm(ks[19], (L, H_C), 0.5),
        'na_bias': nrm(ks[20], (L, H_D, 2 * NA_KH - 1, 2 * NA_KW - 1), 0.1),
        'w_up': nrm(ks[21], (L, D, 2 * D_FF), D ** -0.5),
        'conv_ffn_w': nrm(ks[22], (L, CONV_W, 2 * D_FF), CONV_W ** -0.5),
        'w_down': nrm(ks[23], (L, D_FF, D), D_FF ** -0.5),
    }


def reference(x, c, ctx, c_ctx, w_ada, b_ada, g_attn_pre, g_attn_post, g_mlp_pre, g_mlp_post,
              w_in, w_out, g_q_a, g_k_a, lam_q1, lam_k1, lam_q2, lam_k2, g_sub_b, sink_c, na_bias,
              w_up, conv_ffn_w, w_down):
    S = x.shape[1]
    n_rows = S // GRID_W
    t = jnp.arange(S)
    rows = (t // GRID_W).astype(jnp.float32)
    cols = (t % GRID_W).astype(jnp.float32)
    rope_hd = axial_rope(HEAD_DIM, rows, cols)
    rope_dh = axial_rope(DH_B, rows, cols)
    silu_c = jax.nn.silu(c)
    silu_cc = jax.nn.silu(c_ctx)
    h = ctx
    for l in range(DEPTH):
        x, h = trunk_layer(x, h, silu_c, silu_cc, rope_hd, rope_dh, n_rows, l, l == DEPTH - 1,
                           w_ada[l], b_ada[l], g_attn_pre[l], g_attn_post[l], g_mlp_pre[l], g_mlp_post[l],
                           w_in[l], w_out[l], g_q_a[l], g_k_a[l], lam_q1[l], lam_k1[l], lam_q2[l], lam_k2[l],
                           g_sub_b[l], sink_c[l], na_bias[l], w_up[l], conv_ffn_w[l], w_down[l])
    return x
```

```python
import functools
import math

import jax
import jax.numpy as jnp
import numpy as np
from jax import lax
from jax.experimental import pallas as pl
from jax.experimental.pallas import tpu as pltpu

F32 = jnp.float32
BF16 = jnp.bfloat16

GRID_W = 64
HEAD_DIM = 128
N_GROUP_HEADS = 8
KV_GQA = 2
GQA_G = N_GROUP_HEADS // KV_GQA
DH_B = HEAD_DIM // 2
WIN_C = 128
NA_KH = 8
NA_KW = 16
CONV_W = 3
ROPE_THETA = 10000.0
EPS = 1e-6
NEG_INF = -1e30
GROUP_W = N_GROUP_HEADS * HEAD_DIM
N_Q = 4 * GROUP_W

QA_COL, QB_COL, QC_COL, QD_COL = 0, 8, 16, 24
KA_COL, VA_COL = 32, 34
KB_COL, VB_COL = 36, 44
KC_COL, VC_COL = 52, 54
KD_COL, VD_COL = 56, 64
KV_COLS = 40

LANE = 128
HALO = 16
VMEM_LIMIT = 56 * 1024 * 1024


def _cparams(sem):
    return pltpu.CompilerParams(dimension_semantics=sem, vmem_limit_bytes=VMEM_LIMIT)


def _ada_kernel(a_ref, w_ref, b_ref, o_ref):
    a = a_ref[...]
    a = a * jax.nn.sigmoid(a)
    o_ref[...] = jnp.dot(a.astype(BF16), w_ref[...].astype(BF16),
                         preferred_element_type=F32) + b_ref[...]


def _ada_call(cond, w_ada, b_ada):
    n_layers, d, n = w_ada.shape
    tn = 512
    return pl.pallas_call(
        _ada_kernel,
        out_shape=jax.ShapeDtypeStruct((n_layers, 8, n), F32),
        grid=(n_layers, n // tn),
        in_specs=[pl.BlockSpec((8, d), lambda l, j: (0, 0)),
                  pl.BlockSpec((None, d, tn), lambda l, j: (l, 0, j)),
                  pl.BlockSpec((None, 1, tn), lambda l, j: (l, 0, j))],
        out_specs=pl.BlockSpec((None, 8, tn), lambda l, j: (l, 0, j)),
        compiler_params=_cparams(("parallel", "parallel")),
        name="ada_mod",
    )(cond, w_ada, b_ada.reshape(n_layers, 1, n))


def _rms(x, g):
    return x * lax.rsqrt(jnp.mean(x * x, axis=-1, keepdims=True) + EPS) * g


def _norm_mod_kernel(x_ref, g_ref, sc_ref, sh_ref, o_ref):
    y = _rms(x_ref[...], g_ref[...])
    o_ref[...] = (y * (1.0 + sc_ref[...]) + sh_ref[...]).astype(o_ref.dtype)


def _norm_mod_call(x, g, sc, sh):
    nb, r, d = x.shape
    tr = min(256, r)
    return pl.pallas_call(
        _norm_mod_kernel,
        out_shape=jax.ShapeDtypeStruct((nb, r, d), BF16),
        grid=(nb, r // tr),
        in_specs=[pl.BlockSpec((None, tr, d), lambda b, i: (b, i, 0)),
                  pl.BlockSpec((1, d), lambda b, i: (0, 0)),
                  pl.BlockSpec((None, 1, d), lambda b, i: (b, 0, 0)),
                  pl.BlockSpec((None, 1, d), lambda b, i: (b, 0, 0))],
        out_specs=pl.BlockSpec((None, tr, d), lambda b, i: (b, i, 0)),
        compiler_params=_cparams(("parallel", "parallel")),
        name="norm_mod",
    )(x, g.reshape(1, d), sc.reshape(nb, 1, d), sh.reshape(nb, 1, d))


def _norm_res_kernel(x_ref, y_ref, g_ref, gt_ref, o_ref):
    o_ref[...] = x_ref[...] + gt_ref[...] * _rms(y_ref[...], g_ref[...])


def _norm_res_call(x, y, g, gate):
    nb, r, d = x.shape
    tr = min(256, r)
    return pl.pallas_call(
        _norm_res_kernel,
        out_shape=jax.ShapeDtypeStruct((nb, r, d), F32),
        grid=(nb, r // tr),
        in_specs=[pl.BlockSpec((None, tr, d), lambda b, i: (b, i, 0)),
                  pl.BlockSpec((None, tr, d), lambda b, i: (b, i, 0)),
                  pl.BlockSpec((1, d), lambda b, i: (0, 0)),
                  pl.BlockSpec((None, 1, d), lambda b, i: (b, 0, 0))],
        out_specs=pl.BlockSpec((None, tr, d), lambda b, i: (b, i, 0)),
        compiler_params=_cparams(("parallel", "parallel")),
        name="norm_res",
    )(x, y.reshape(nb, r, d), g.reshape(1, d), gate.reshape(nb, 1, d))


def _matmul_kernel(a_ref, w_ref, o_ref, acc_ref, *, nk):
    k = pl.program_id(2)
    part = jnp.dot(a_ref[...], w_ref[...], preferred_element_type=F32)
    if nk == 1:
        o_ref[...] = part.astype(o_ref.dtype)
    else:
        @pl.when(k == 0)
        def _():
            acc_ref[...] = part

        @pl.when(jnp.logical_and(k > 0, k < nk - 1))
        def _():
            acc_ref[...] += part

        @pl.when(k == nk - 1)
        def _():
            o_ref[...] = (acc_ref[...] + part).astype(o_ref.dtype)


def _pick(n, prefs):
    for t in prefs:
        if n % t == 0:
            return t
    return n


def _matmul_call(a, w, out_dtype):
    m, kdim = a.shape
    _, n = w.shape
    tm = _pick(m, (1024, 512, 256))
    tn = _pick(n, (512, 256, 128))
    tk = kdim if kdim <= 4096 else _pick(kdim, (5504, 4096, 2048))
    nk = kdim // tk
    return pl.pallas_call(
        functools.partial(_matmul_kernel, nk=nk),
        out_shape=jax.ShapeDtypeStruct((m, n), out_dtype),
        grid=(m // tm, n // tn, nk),
        in_specs=[pl.BlockSpec((tm, tk), lambda i, j, k: (i, k)),
                  pl.BlockSpec((tk, tn), lambda i, j, k: (k, j))],
        out_specs=pl.BlockSpec((tm, tn), lambda i, j, k: (i, j)),
        scratch_shapes=[pltpu.VMEM((tm, tn) if nk > 1 else (8, LANE), F32)],
        compiler_params=_cparams(("parallel", "parallel", "arbitrary")),
        name="matmul",
    )(a, w)


def _swap_pairs(y):
    lane = lax.broadcasted_iota(jnp.int32, y.shape, 1)
    nxt = pltpu.roll(y, LANE - 1, 1)
    prv = pltpu.roll(y, 1, 1)
    return jnp.where((lane & 1) == 0, nxt, prv)


def _prep_kernel(p_ref, cos_ref, sin_ref, g_ref, o_ref, *, n_chunks, norm, rope, scale):
    for c in range(n_chunks):
        sl = slice(c * LANE, (c + 1) * LANE)
        y = p_ref[:, sl].astype(F32)
        if norm:
            y = _rms(y, g_ref[...])
        if rope:
            y = y * cos_ref[...] + _swap_pairs(y) * sin_ref[...]
        if scale != 1.0:
            y = y * scale
        o_ref[:, sl] = y.astype(o_ref.dtype)


def _prep_call(p, col0, width, seq, *, gain=None, rope=None, scale=1.0):
    rows = p.shape[0]
    tr = min(256, seq)
    n_seq_tiles = seq // tr
    bw = min(width, 512)
    assert (col0 * LANE) % bw == 0 and width % bw == 0
    cb0 = col0 * LANE // bw
    if rope is None:
        cos = sin = jnp.zeros((tr, LANE), F32)
        tab_map = lambda i, j: (0, 0)
    else:
        cos, sin = rope
        tab_map = lambda i, j: (i % n_seq_tiles, 0)
    g = (jnp.ones((HEAD_DIM,), F32) if gain is None else gain).reshape(1, HEAD_DIM)
    kern = functools.partial(_prep_kernel, n_chunks=bw // LANE, norm=gain is not None,
                             rope=rope is not None, scale=float(scale))
    return pl.pallas_call(
        kern,
        out_shape=jax.ShapeDtypeStruct((rows, width), BF16),
        grid=(rows // tr, width // bw),
        in_specs=[pl.BlockSpec((tr, bw), lambda i, j: (i, cb0 + j)),
                  pl.BlockSpec((tr, LANE), tab_map),
                  pl.BlockSpec((tr, LANE), tab_map),
                  pl.BlockSpec((1, HEAD_DIM), lambda i, j: (0, 0))],
        out_specs=pl.BlockSpec((tr, bw), lambda i, j: (i, j)),
        compiler_params=_cparams(("parallel", "parallel")),
        name="qk_prep",
    )(p, cos, sin, g)


def _rope_tables(seq, head_dim):
    t = jnp.arange(seq)
    rows = (t // GRID_W).astype(F32)
    cols = (t % GRID_W).astype(F32)
    quarter = head_dim // 4
    inv = ROPE_THETA ** (-jnp.arange(quarter, dtype=F32) / quarter)
    ang = jnp.concatenate([rows[:, None] * inv, cols[:, None] * inv], axis=-1)
    lane = np.arange(LANE)
    pair = (lane % head_dim) // 2
    sign = np.where(lane % 2 == 0, -1.0, 1.0).astype(np.float32)
    return jnp.cos(ang)[:, pair], jnp.sin(ang)[:, pair] * sign


def _qk(q, k):
    return lax.dot_general(q, k, (((1,), (1,)), ((), ())), preferred_element_type=F32)


def _stack_heads(q, n):
    if n == 1:
        return q
    return jnp.concatenate([q[:, g * LANE:(g + 1) * LANE] for g in range(n)], axis=0)


def _unstack_heads(o, n):
    if n == 1:
        return o
    t = o.shape[0] // n
    return jnp.concatenate([o[g * t:(g + 1) * t] for g in range(n)], axis=1)


def _stack_diff(q):
    lane = lax.broadcasted_iota(jnp.int32, q.shape, 1)
    zero = jnp.zeros_like(q)
    return jnp.concatenate([jnp.where(lane < DH_B, q, zero),
                            jnp.where(lane >= DH_B, q, zero)], axis=0)


def _softmax_attend(groups, sink=None):
    m = None
    for s, _ in groups:
        sm = jnp.max(s, axis=-1, keepdims=True)
        m = sm if m is None else jnp.maximum(m, sm)
    if sink is not None:
        m = jnp.maximum(m, sink)
    l = None
    acc = None
    for s, v in groups:
        p = jnp.exp(s - m)
        ps = jnp.sum(p, axis=-1, keepdims=True)
        pv = jnp.dot(p.astype(v.dtype), v, preferred_element_type=F32)
        l = ps if l is None else l + ps
        acc = pv if acc is None else acc + pv
    if sink is not None:
        l = l + jnp.exp(sink - m)
    return acc / l


def _diff_lambda(lam_ref, lam_init):
    a = jnp.sum(lam_ref[0:1, :] * lam_ref[1:2, :], axis=-1, keepdims=True)
    b = jnp.sum(lam_ref[2:3, :] * lam_ref[3:4, :], axis=-1, keepdims=True)
    return jnp.exp(a) - jnp.exp(b) + lam_init


def _diff_finalize(o, lam_ref, gsub_ref, lam_init):
    t = o.shape[0] // 2
    d = o[:t] - _diff_lambda(lam_ref, lam_init) * o[t:]
    return _rms(d, gsub_ref[...]) * (1.0 - lam_init)


def _sink_rows(sink_ref, kv, rows_per_head, n_rows):
    head = lax.broadcasted_iota(jnp.int32, (n_rows, 1), 0) // rows_per_head
    out = jnp.zeros((n_rows, 1), F32)
    for g in range(GQA_G):
        out = jnp.where(head == g, sink_ref[kv * GQA_G + g], out)
    return out


def _flash_kernel(q_ref, kl_ref, vl_ref, kc_ref, vc_ref, lam_ref, gsub_ref, o_ref,
                  m_ref, l_ref, acc_ref, *, diff, n_stack, tk, lam_init):
    q = q_ref[...]
    q = _stack_diff(q) if diff else _stack_heads(q, n_stack)
    m_ref[...] = jnp.full(m_ref.shape, -jnp.inf, F32)
    l_ref[...] = jnp.zeros(l_ref.shape, F32)
    acc_ref[...] = jnp.zeros(acc_ref.shape, F32)

    def step(k, v):
        s = _qk(q, k)
        m_prev = m_ref[...]
        m_new = jnp.maximum(m_prev, jnp.max(s, axis=-1, keepdims=True))
        alpha = jnp.exp(m_prev - m_new)
        p = jnp.exp(s - m_new)
        l_ref[...] = alpha * l_ref[...] + jnp.sum(p, axis=-1, keepdims=True)
        acc_ref[...] = alpha * acc_ref[...] + jnp.dot(p.astype(v.dtype), v,
                                                      preferred_element_type=F32)
        m_ref[...] = m_new

    def body(c, carry):
        off = pl.multiple_of(c * tk, tk)
        step(kl_ref[pl.ds(off, tk), :], vl_ref[pl.ds(off, tk), :])
        return carry

    lax.fori_loop(0, kl_ref.shape[0] // tk, body, 0)
    step(kc_ref[...], vc_ref[...])
    o = acc_ref[...] / l_ref[...]
    if diff:
        o = _diff_finalize(o, lam_ref, gsub_ref, lam_init)
    else:
        o = _unstack_heads(o, n_stack)
    o_ref[...] = o.astype(o_ref.dtype)


def _flash_call(q, kl, kl_col, vl, vl_col, kc, kc_col, vc, vc_col, *, batch, seq, ctx,
                diff, lam_rows=None, g_sub=None, lam_init=0.0):
    tq = 128
    n_heads = N_GROUP_HEADS if diff else KV_GQA
    n_stack = 2 if diff else GQA_G
    qw = LANE if diff else GQA_G * LANE
    nq = seq // tq
    rows = n_stack * tq
    if lam_rows is None:
        lam_rows = jnp.zeros((8, LANE), F32)
        g_sub = jnp.ones((HEAD_DIM,), F32)
    kern = functools.partial(_flash_kernel, diff=diff, n_stack=n_stack, tk=512,
                             lam_init=float(lam_init))
    return pl.pallas_call(
        kern,
        out_shape=jax.ShapeDtypeStruct((batch * seq, GROUP_W), BF16),
        grid=(batch, n_heads, nq),
        in_specs=[pl.BlockSpec((tq, qw), lambda b, h, i: (b * nq + i, h)),
                  pl.BlockSpec((seq, LANE), lambda b, h, i: (b, kl_col + h)),
                  pl.BlockSpec((seq, LANE), lambda b, h, i: (b, vl_col + h)),
                  pl.BlockSpec((ctx, LANE), lambda b, h, i: (b, kc_col + h)),
                  pl.BlockSpec((ctx, LANE), lambda b, h, i: (b, vc_col + h)),
                  pl.BlockSpec((8, LANE), lambda b, h, i: (0, 0)),
                  pl.BlockSpec((1, HEAD_DIM), lambda b, h, i: (0, 0))],
        out_specs=pl.BlockSpec((tq, qw), lambda b, h, i: (b * nq + i, h)),
        scratch_shapes=[pltpu.VMEM((rows, 1), F32), pltpu.VMEM((rows, 1), F32),
                        pltpu.VMEM((rows, HEAD_DIM), F32)],
        compiler_params=_cparams(("parallel", "parallel", "parallel")),
        name="attn_full",
    )(q, kl, vl, kc, vc, lam_rows, g_sub.reshape(1, HEAD_DIM))


def _band_kernel(sink_ref, q_ref, kp_ref, kx_ref, kn_ref, vp_ref, vx_ref, vn_ref,
                 kc_ref, vc_ref, o_ref, *, tq, seq):
    kv = pl.program_id(1)
    i = pl.program_id(2)
    q = _stack_heads(q_ref[...], GQA_G)
    rows = q.shape[0]
    k_loc = jnp.concatenate([kp_ref[...], kx_ref[...], kn_ref[...]], axis=0)
    v_loc = jnp.concatenate([vp_ref[...], vx_ref[...], vn_ref[...]], axis=0)
    s_loc = _qk(q, k_loc)
    qpos = i * tq + (lax.broadcasted_iota(jnp.int32, s_loc.shape, 0) & (tq - 1))
    kpos = (i - 1) * tq + lax.broadcasted_iota(jnp.int32, s_loc.shape, 1)
    ok = jnp.where(kpos >= 0, jnp.abs(qpos - kpos), WIN_C + 1)
    ok = jnp.where(kpos < seq, ok, WIN_C + 1) <= WIN_C
    s_loc = jnp.where(ok, s_loc, NEG_INF)
    s_ctx = _qk(q, kc_ref[...])
    sink = _sink_rows(sink_ref, kv, tq, rows)
    o = _softmax_attend([(s_loc, v_loc), (s_ctx, vc_ref[...])], sink)
    o_ref[...] = _unstack_heads(o, GQA_G).astype(o_ref.dtype)


def _band_call(sink, q, kl, p, pc, pc_off, *, batch, seq, ctx):
    tq = WIN_C
    nq = seq // tq
    qw = GQA_G * LANE

    def nb(delta, col):
        def imap(b, h, i):
            return (b * nq + jnp.clip(i + delta, 0, nq - 1), col + h)
        return imap

    return pl.pallas_call(
        functools.partial(_band_kernel, tq=tq, seq=seq),
        out_shape=jax.ShapeDtypeStruct((batch * seq, GROUP_W), BF16),
        grid=(batch, KV_GQA, nq),
        in_specs=[pl.BlockSpec(memory_space=pltpu.SMEM),
                  pl.BlockSpec((tq, qw), lambda b, h, i: (b * nq + i, h)),
                  pl.BlockSpec((tq, LANE), nb(-1, 0)),
                  pl.BlockSpec((tq, LANE), nb(0, 0)),
                  pl.BlockSpec((tq, LANE), nb(1, 0)),
                  pl.BlockSpec((tq, LANE), nb(-1, VC_COL)),
                  pl.BlockSpec((tq, LANE), nb(0, VC_COL)),
                  pl.BlockSpec((tq, LANE), nb(1, VC_COL)),
                  pl.BlockSpec((ctx, LANE), lambda b, h, i: (b, KC_COL - pc_off + h)),
                  pl.BlockSpec((ctx, LANE), lambda b, h, i: (b, VC_COL - pc_off + h))],
        out_specs=pl.BlockSpec((tq, qw), lambda b, h, i: (b * nq + i, h)),
        compiler_params=_cparams(("parallel", "parallel", "parallel")),
        name="attn_band",
    )(sink, q, kl, kl, kl, p, p, p, pc, pc)


def _nbr_kernel(q_ref, k_ref, v_ref, kc_ref, vc_ref, bias_ref, o_ref, *, n_rows, scale):
    slab = NA_KH * GRID_W
    kc = kc_ref[...]
    vc = vc_ref[...]

    def body(r, carry):
        rs = jnp.clip(r - NA_KH // 2, 0, n_rows - NA_KH)
        var = rs - r + NA_KH - 1
        q = q_ref[pl.ds(pl.multiple_of(r * GRID_W, GRID_W), GRID_W), :]
        off = pl.multiple_of(rs * GRID_W, GRID_W)
        ks = k_ref[pl.ds(off, slab), :]
        vs = v_ref[pl.ds(off, slab), :]
        s_loc = _qk(q, ks) * scale + bias_ref[var]
        s_ctx = _qk(q, kc) * scale
        o = _softmax_attend([(s_loc, vs), (s_ctx, vc)])
        o_ref[pl.ds(pl.multiple_of(r * GRID_W, GRID_W), GRID_W), :] = o.astype(o_ref.dtype)
        return carry

    lax.fori_loop(0, n_rows, body, 0)


def _nbr_bias(na_bias, n_rows):
    assert n_rows >= NA_KH
    w = np.arange(GRID_W)[:, None]
    c = np.arange(GRID_W)[None, :]
    cs = np.clip(w - NA_KW // 2, 0, GRID_W - NA_KW)
    inwin = (c >= cs) & (c < cs + NA_KW)
    coff = np.clip(c - w + NA_KW - 1, 0, 2 * NA_KW - 2)
    roff = np.arange(NA_KH)[:, None] + np.arange(NA_KH)[None, :]
    dense = na_bias[:, roff[:, None, :, None], coff[None, :, None, :]]
    dense = jnp.where(inwin[None, None, :, None, :], dense.astype(F32), NEG_INF)
    return dense.reshape(na_bias.shape[0], NA_KH, GRID_W, NA_KH * GRID_W)


def _nbr_call(p, pc, pc_off, bias, *, batch, seq, ctx):
    n_rows = seq // GRID_W
    scale = HEAD_DIM ** -0.5
    return pl.pallas_call(
        functools.partial(_nbr_kernel, n_rows=n_rows, scale=scale),
        out_shape=jax.ShapeDtypeStruct((batch * seq, GROUP_W), BF16),
        grid=(batch, N_GROUP_HEADS),
        in_specs=[pl.BlockSpec((seq, LANE), lambda b, h: (b, QD_COL + h)),
                  pl.BlockSpec((seq, LANE), lambda b, h: (b, KD_COL + h)),
                  pl.BlockSpec((seq, LANE), lambda b, h: (b, VD_COL + h)),
                  pl.BlockSpec((ctx, LANE), lambda b, h: (b, KD_COL - pc_off + h)),
                  pl.BlockSpec((ctx, LANE), lambda b, h: (b, VD_COL - pc_off + h)),
                  pl.BlockSpec((None, NA_KH, GRID_W, NA_KH * GRID_W),
                               lambda b, h: (h, 0, 0, 0))],
        out_specs=pl.BlockSpec((seq, LANE), lambda b, h: (b, h)),
        compiler_params=_cparams(("parallel", "parallel")),
        name="attn_nbr",
    )(p, p, p, pc, pc, bias)


def _ctx_kernel(sink_ref, q_ref, k_ref, v_ref, lam_ref, gsub_ref, o_ref, *,
                diff, n_stack, scale, use_sink, lam_init):
    q = q_ref[...]
    q = _stack_diff(q) if diff else _stack_heads(q, n_stack)
    s = _qk(q, k_ref[...])
    if scale != 1.0:
        s = s * scale
    sink = None
    if use_sink:
        sink = _sink_rows(sink_ref, pl.program_id(1), q_ref.shape[0], q.shape[0])
    o = _softmax_attend([(s, v_ref[...])], sink)
    if diff:
        o = _diff_finalize(o, lam_ref, gsub_ref, lam_init)
    else:
        o = _unstack_heads(o, n_stack)
    o_ref[...] = o.astype(o_ref.dtype)


def _ctx_call(q, q_col, k, k_col, v, v_col, *, batch, ctx, n_heads, n_stack, diff=False,
              scale=1.0, sink=None, lam_rows=None, g_sub=None, lam_init=0.0):
    qw = LANE if diff else n_stack * LANE
    qb = q_col * LANE // qw
    if lam_rows is None:
        lam_rows = jnp.zeros((8, LANE), F32)
        g_sub = jnp.ones((HEAD_DIM,), F32)
    use_sink = sink is not None
    if sink is None:
        sink = jnp.zeros((N_GROUP_HEADS,), F32)
    kern = functools.partial(_ctx_kernel, diff=diff, n_stack=n_stack, scale=float(scale),
                             use_sink=use_sink, lam_init=float(lam_init))
    return pl.pallas_call(
        kern,
        out_shape=jax.ShapeDtypeStruct((batch * ctx, GROUP_W), BF16),
        grid=(batch, n_heads),
        in_specs=[pl.BlockSpec(memory_space=pltpu.SMEM),
                  pl.BlockSpec((ctx, qw), lambda b, h: (b, qb + h)),
                  pl.BlockSpec((ctx, LANE), lambda b, h: (b, k_col + h)),
                  pl.BlockSpec((ctx, LANE), lambda b, h: (b, v_col + h)),
                  pl.BlockSpec((8, LANE), lambda b, h: (0, 0)),
                  pl.BlockSpec((1, HEAD_DIM), lambda b, h: (0, 0))],
        out_specs=pl.BlockSpec((ctx, qw), lambda b, h: (b, h)),
        compiler_params=_cparams(("parallel", "parallel")),
        name="attn_ctx",
    )(sink, q, k, v, lam_rows, g_sub.reshape(1, HEAD_DIM))


def _conv_gate_kernel(ug_ref, ugp_ref, ugn_ref, uv_ref, uvp_ref, uvn_ref, cwg_ref, cwv_ref,
                      o_ref, *, tiles_per_seq):
    i = pl.program_id(0)
    first = (i % tiles_per_seq) == 0
    last = (i % tiles_per_seq) == tiles_per_seq - 1

    def conv(u_ref, up_ref, un_ref, cw_ref):
        u = u_ref[...].astype(F32)
        tr = u.shape[0]
        row = lax.broadcasted_iota(jnp.int32, u.shape, 0)
        prev_row = jnp.where(first, 0.0, up_ref[HALO - 1:HALO, :].astype(F32))
        next_row = jnp.where(last, 0.0, un_ref[0:1, :].astype(F32))
        below = jnp.where(row == 0, prev_row, pltpu.roll(u, 1, 0))
        above = jnp.where(row == tr - 1, next_row, pltpu.roll(u, tr - 1, 0))
        return below * cw_ref[0:1, :] + u * cw_ref[1:2, :] + above * cw_ref[2:3, :]

    gate = conv(ug_ref, ugp_ref, ugn_ref, cwg_ref)
    val = conv(uv_ref, uvp_ref, uvn_ref, cwv_ref)
    o_ref[...] = (gate * jax.nn.sigmoid(gate) * val).astype(o_ref.dtype)


def _conv_gate_call(u, conv_w, seq):
    rows, n2 = u.shape
    f = n2 // 2
    tr = min(256, seq)
    tc = _pick(f, (512, 256, 128))
    tiles_per_seq = seq // tr
    n_tiles = rows // tr
    fb = f // tc
    hb = tr // HALO
    cw = jnp.pad(conv_w, ((0, 8 - CONV_W), (0, 0)))

    def prev_map(off):
        return lambda i, j: (jnp.maximum(i * hb - 1, 0), off + j)

    def next_map(off):
        return lambda i, j: (jnp.minimum((i + 1) * hb, n_tiles * hb - 1), off + j)

    return pl.pallas_call(
        functools.partial(_conv_gate_kernel, tiles_per_seq=tiles_per_seq),
        out_shape=jax.ShapeDtypeStruct((rows, f), BF16),
        grid=(n_tiles, fb),
        in_specs=[pl.BlockSpec((tr, tc), lambda i, j: (i, j)),
                  pl.BlockSpec((HALO, tc), prev_map(0)),
                  pl.BlockSpec((HALO, tc), next_map(0)),
                  pl.BlockSpec((tr, tc), lambda i, j: (i, fb + j)),
                  pl.BlockSpec((HALO, tc), prev_map(fb)),
                  pl.BlockSpec((HALO, tc), next_map(fb)),
                  pl.BlockSpec((8, tc), lambda i, j: (0, j)),
                  pl.BlockSpec((8, tc), lambda i, j: (0, fb + j))],
        out_specs=pl.BlockSpec((tr, tc), lambda i, j: (i, j)),
        compiler_params=_cparams(("parallel", "parallel")),
        name="conv_gate",
    )(u, u, u, u, u, u, cw, cw)


def _conv_ffn(hx2d, w_up, conv_w, w_down, seq):
    u = _matmul_call(hx2d, w_up, BF16)
    g = _conv_gate_call(u, conv_w, seq)
    return _matmul_call(g, w_down, F32)


def kernel(x, c, ctx, c_ctx, w_ada, b_ada, g_attn_pre, g_attn_post, g_mlp_pre, g_mlp_post,
           w_in, w_out, g_q_a, g_k_a, lam_q1, lam_k1, lam_q2, lam_k2, g_sub_b, sink_c, na_bias,
           w_up, conv_ffn_w, w_down):
    batch, seq, d = x.shape
    n_ctx = ctx.shape[1]
    depth = w_ada.shape[0]
    n_rows = seq // GRID_W
    assert batch + 1 <= 8 and d == N_Q

    cond = jnp.zeros((8, d), F32).at[:batch].set(c).at[batch].set(c_ctx)
    mod_all = _ada_call(cond, w_ada, b_ada)
    rope_hd = _rope_tables(seq, HEAD_DIM)
    rope_dh = _rope_tables(seq, DH_B)
    scale_hd = HEAD_DIM ** -0.5
    scale_dh = DH_B ** -0.5

    h = ctx
    for l in range(depth):
        last = l == depth - 1
        lam_init = 0.8 - 0.6 * math.exp(-0.3 * l)
        mod = mod_all[l, :batch].reshape(batch, 6, d)
        sh_a, sc_a, gt_a, sh_m, sc_m, gt_m = (mod[:, k] for k in range(6))
        mod_c = mod_all[l, batch].reshape(6, d)
        one = lambda v: v.reshape(1, d)
        wi = w_in[l].astype(BF16)
        wo = w_out[l].astype(BF16)
        wu = w_up[l].astype(BF16)
        wd = w_down[l].astype(BF16)
        lam_rows = jnp.zeros((8, LANE), F32).at[:4, :DH_B].set(
            jnp.stack([lam_q1[l], lam_k1[l], lam_q2[l], lam_k2[l]]))

        hx = _norm_mod_call(x, g_attn_pre[l], sc_a, sh_a).reshape(batch * seq, d)
        hc = _norm_mod_call(h.reshape(1, batch * n_ctx, d), g_attn_pre[l],
                            one(mod_c[1]), one(mod_c[0])).reshape(batch * n_ctx, d)
        p = _matmul_call(hx, wi, BF16)
        pc_off = 32 if last else 0
        pc = _matmul_call(hc, wi[:, N_Q:] if last else wi, BF16)

        q_a = _prep_call(p, QA_COL, GROUP_W, seq, gain=g_q_a[l], rope=rope_hd, scale=scale_hd)
        k_a = _prep_call(p, KA_COL, KV_GQA * LANE, seq, gain=g_k_a[l], rope=rope_hd)
        kc_a = _prep_call(pc, KA_COL - pc_off, KV_GQA * LANE, n_ctx, gain=g_k_a[l])
        q_b = _prep_call(p, QB_COL, GROUP_W, seq, rope=rope_dh, scale=scale_dh)
        k_b = _prep_call(p, KB_COL, GROUP_W, seq, rope=rope_dh)
        q_c = _prep_call(p, QC_COL, GROUP_W, seq, rope=rope_hd, scale=scale_hd)
        k_c = _prep_call(p, KC_COL, KV_GQA * LANE, seq, rope=rope_hd)

        dims = dict(batch=batch, seq=seq, ctx=n_ctx)
        o_a = _flash_call(q_a, k_a, 0, p, VA_COL, kc_a, 0, pc, VA_COL - pc_off, diff=False, **dims)
        o_b = _flash_call(q_b, k_b, 0, p, VB_COL, pc, KB_COL - pc_off, pc, VB_COL - pc_off,
                          diff=True, lam_rows=lam_rows, g_sub=g_sub_b[l], lam_init=lam_init, **dims)
        o_c = _band_call(sink_c[l], q_c, k_c, p, pc, pc_off, **dims)
        o_d = _nbr_call(p, pc, pc_off, _nbr_bias(na_bias[l], n_rows), **dims)
        o = jnp.concatenate([o_a, o_b, o_c, o_d], axis=-1)
        y = _matmul_call(o, wo, F32)
        x = _norm_res_call(x, y, g_attn_post[l], gt_a)

        hx = _norm_mod_call(x, g_mlp_pre[l], sc_m, sh_m).reshape(batch * seq, d)
        y = _conv_ffn(hx, wu, conv_ffn_w[l], wd, seq)
        x = _norm_res_call(x, y, g_mlp_post[l], gt_m)

        if not last:
            cdims = dict(batch=batch, ctx=n_ctx)
            qc_a = _prep_call(pc, QA_COL, GROUP_W, n_ctx, gain=g_q_a[l], scale=scale_hd)
            co_a = _ctx_call(qc_a, 0, kc_a, 0, pc, VA_COL, n_heads=KV_GQA, n_stack=GQA_G, **cdims)
            co_b = _ctx_call(pc, QB_COL, pc, KB_COL, pc, VB_COL, n_heads=N_GROUP_HEADS,
                             n_stack=2, diff=True, scale=scale_dh, lam_rows=lam_rows,
                             g_sub=g_sub_b[l], lam_init=lam_init, **cdims)
            co_c = _ctx_call(pc, QC_COL, pc, KC_COL, pc, VC_COL, n_heads=KV_GQA, n_stack=GQA_G,
                             scale=scale_hd, sink=sink_c[l], **cdims)
            co_d = _ctx_call(pc, QD_COL, pc, KD_COL, pc, VD_COL, n_heads=N_GROUP_HEADS,
                             n_stack=1, scale=scale_hd, **cdims)
            co = jnp.concatenate([co_a, co_b, co_c, co_d], axis=-1)
            yc = _matmul_call(co, wo, F32)
            h3 = h.reshape(1, batch * n_ctx, d)
            h3 = _norm_res_call(h3, yc, g_attn_post[l], one(mod_c[2]))
            hc = _norm_mod_call(h3, g_mlp_pre[l], one(mod_c[4]), one(mod_c[3]))
            yc = _conv_ffn(hc.reshape(batch * n_ctx, d), wu, conv_ffn_w[l], wd, n_ctx)
            h = _norm_res_call(h3, yc, g_mlp_post[l], one(mod_c[5])).reshape(batch, n_ctx, d)
    return x
```

```python
import functools
import math

import jax
import jax.numpy as jnp
import numpy as np
from jax import lax
from jax.experimental import pallas as pl
from jax.experimental.pallas import tpu as pltpu

F32 = jnp.float32
BF16 = jnp.bfloat16

GRID_W = 64
HEAD_DIM = 128
N_GROUP_HEADS = 8
KV_GQA = 2
GQA_G = N_GROUP_HEADS // KV_GQA
DH_B = HEAD_DIM // 2
WIN_C = 128
NA_KH = 8
NA_KW = 16
CONV_W = 3
ROPE_THETA = 10000.0
EPS = 1e-6
NEG_INF = -1e30
LOG2E = math.log2(math.e)
GROUP_W = N_GROUP_HEADS * HEAD_DIM
N_Q = 4 * GROUP_W

QA_COL, QB_COL, QC_COL, QD_COL = 0, 8, 16, 24
KA_COL, VA_COL = 32, 34
KB_COL, VB_COL = 36, 44
KC_COL, VC_COL = 52, 54
KD_COL, VD_COL = 56, 64
KV_COLS = 40

LANE = 128
HALO = 16
VMEM_LIMIT = 56 * 1024 * 1024


def _cparams(sem):
    return pltpu.CompilerParams(dimension_semantics=sem, vmem_limit_bytes=VMEM_LIMIT)


def _ada_kernel(a_ref, w_ref, b_ref, o_ref):
    a = a_ref[...]
    a = a * jax.nn.sigmoid(a)
    o_ref[...] = jnp.dot(a.astype(BF16), w_ref[...].astype(BF16),
                         preferred_element_type=F32) + b_ref[...]


def _ada_call(cond, w_ada, b_ada):
    n_layers, d, n = w_ada.shape
    tn = 512
    return pl.pallas_call(
        _ada_kernel,
        out_shape=jax.ShapeDtypeStruct((n_layers, 8, n), F32),
        grid=(n_layers, n // tn),
        in_specs=[pl.BlockSpec((8, d), lambda l, j: (0, 0)),
                  pl.BlockSpec((None, d, tn), lambda l, j: (l, 0, j)),
                  pl.BlockSpec((None, 1, tn), lambda l, j: (l, 0, j))],
        out_specs=pl.BlockSpec((None, 8, tn), lambda l, j: (l, 0, j)),
        compiler_params=_cparams(("parallel", "parallel")),
        name="ada_mod",
    )(cond, w_ada, b_ada.reshape(n_layers, 1, n))


def _rms(x, g):
    return x * lax.rsqrt(jnp.mean(x * x, axis=-1, keepdims=True) + EPS) * g


def _norm_mod_kernel(x_ref, g_ref, sc_ref, sh_ref, o_ref):
    y = _rms(x_ref[...], g_ref[...])
    o_ref[...] = (y * (1.0 + sc_ref[...]) + sh_ref[...]).astype(o_ref.dtype)


def _norm_mod_call(x, g, sc, sh):
    nb, r, d = x.shape
    tr = min(256, r)
    return pl.pallas_call(
        _norm_mod_kernel,
        out_shape=jax.ShapeDtypeStruct((nb, r, d), BF16),
        grid=(nb, r // tr),
        in_specs=[pl.BlockSpec((None, tr, d), lambda b, i: (b, i, 0)),
                  pl.BlockSpec((1, d), lambda b, i: (0, 0)),
                  pl.BlockSpec((None, 1, d), lambda b, i: (b, 0, 0)),
                  pl.BlockSpec((None, 1, d), lambda b, i: (b, 0, 0))],
        out_specs=pl.BlockSpec((None, tr, d), lambda b, i: (b, i, 0)),
        compiler_params=_cparams(("parallel", "parallel")),
        name="norm_mod",
    )(x, g.reshape(1, d), sc.reshape(nb, 1, d), sh.reshape(nb, 1, d))


def _norm_res_kernel(x_ref, y_ref, g_ref, gt_ref, o_ref):
    o_ref[...] = x_ref[...] + gt_ref[...] * _rms(y_ref[...], g_ref[...])


def _norm_res_call(x, y, g, gate):
    nb, r, d = x.shape
    tr = min(256, r)
    return pl.pallas_call(
        _norm_res_kernel,
        out_shape=jax.ShapeDtypeStruct((nb, r, d), F32),
        grid=(nb, r // tr),
        in_specs=[pl.BlockSpec((None, tr, d), lambda b, i: (b, i, 0)),
                  pl.BlockSpec((None, tr, d), lambda b, i: (b, i, 0)),
                  pl.BlockSpec((1, d), lambda b, i: (0, 0)),
                  pl.BlockSpec((None, 1, d), lambda b, i: (b, 0, 0))],
        out_specs=pl.BlockSpec((None, tr, d), lambda b, i: (b, i, 0)),
        compiler_params=_cparams(("parallel", "parallel")),
        name="norm_res",
    )(x, y.reshape(nb, r, d), g.reshape(1, d), gate.reshape(nb, 1, d))


def _matmul_kernel(a_ref, w_ref, o_ref, acc_ref, *, nk):
    k = pl.program_id(2)
    part = jnp.dot(a_ref[...], w_ref[...], preferred_element_type=F32)
    if nk == 1:
        o_ref[...] = part.astype(o_ref.dtype)
    else:
        @pl.when(k == 0)
        def _():
            acc_ref[...] = part

        @pl.when(jnp.logical_and(k > 0, k < nk - 1))
        def _():
            acc_ref[...] += part

        @pl.when(k == nk - 1)
        def _():
            o_ref[...] = (acc_ref[...] + part).astype(o_ref.dtype)


def _pick(n, prefs):
    for t in prefs:
        if n % t == 0:
            return t
    return n


def _matmul_call(a, w, out_dtype):
    m, kdim = a.shape
    _, n = w.shape
    tm = _pick(m, (1024, 512, 256))
    tn = _pick(n, (512, 256, 128))
    tk = kdim if kdim <= 4096 else _pick(kdim, (5504, 4096, 2048))
    nk = kdim // tk
    return pl.pallas_call(
        functools.partial(_matmul_kernel, nk=nk),
        out_shape=jax.ShapeDtypeStruct((m, n), out_dtype),
        grid=(m // tm, n // tn, nk),
        in_specs=[pl.BlockSpec((tm, tk), lambda i, j, k: (i, k)),
                  pl.BlockSpec((tk, tn), lambda i, j, k: (k, j))],
        out_specs=pl.BlockSpec((tm, tn), lambda i, j, k: (i, j)),
        scratch_shapes=[pltpu.VMEM((tm, tn) if nk > 1 else (8, LANE), F32)],
        compiler_params=_cparams(("parallel", "parallel", "arbitrary")),
        name="matmul",
    )(a, w)


def _matmul_ws_kernel(a_ref, w_ref, o_ref, wb_ref):
    @pl.when(pl.program_id(1) == 0)
    def _():
        wb_ref[...] = w_ref[...].astype(BF16)

    o_ref[...] = jnp.dot(a_ref[...], wb_ref[...], preferred_element_type=F32).astype(o_ref.dtype)


def _matmul_ws_call(a, w, out_dtype, col0=0, n_cols=None):
    m, kdim = a.shape
    n = w.shape[1] - col0 if n_cols is None else n_cols
    tm = _pick(m, (1024, 512, 256))
    tn = _pick(n, (512, 256, 128))
    assert col0 % tn == 0
    cb0 = col0 // tn
    return pl.pallas_call(
        _matmul_ws_kernel,
        out_shape=jax.ShapeDtypeStruct((m, n), out_dtype),
        grid=(n // tn, m // tm),
        in_specs=[pl.BlockSpec((tm, kdim), lambda j, i: (i, 0)),
                  pl.BlockSpec((kdim, tn), lambda j, i: (0, cb0 + j))],
        out_specs=pl.BlockSpec((tm, tn), lambda j, i: (i, j)),
        scratch_shapes=[pltpu.VMEM((kdim, tn), BF16)],
        compiler_params=_cparams(("parallel", "arbitrary")),
        name="matmul_ws",
    )(a, w)


def _swap_pairs(y):
    lane = lax.broadcasted_iota(jnp.int32, y.shape, 1)
    nxt = pltpu.roll(y, LANE - 1, 1)
    prv = pltpu.roll(y, 1, 1)
    return jnp.where((lane & 1) == 0, nxt, prv)


def _prep_kernel(p_ref, cos_ref, sin_ref, g_ref, o_ref, *, n_chunks, norm, rope, scale):
    for c in range(n_chunks):
        sl = slice(c * LANE, (c + 1) * LANE)
        y = p_ref[:, sl].astype(F32)
        if norm:
            y = _rms(y, g_ref[...])
        if rope:
            y = y * cos_ref[...] + _swap_pairs(y) * sin_ref[...]
        if scale != 1.0:
            y = y * scale
        o_ref[:, sl] = y.astype(o_ref.dtype)


def _prep_call(p, col0, width, seq, *, gain=None, rope=None, scale=1.0):
    rows = p.shape[0]
    tr = min(256, seq)
    n_seq_tiles = seq // tr
    bw = min(width, 512)
    assert (col0 * LANE) % bw == 0 and width % bw == 0
    cb0 = col0 * LANE // bw
    if rope is None:
        cos = sin = jnp.zeros((tr, LANE), F32)
        tab_map = lambda i, j: (0, 0)
    else:
        cos, sin = rope
        tab_map = lambda i, j: (i % n_seq_tiles, 0)
    g = (jnp.ones((HEAD_DIM,), F32) if gain is None else gain).reshape(1, HEAD_DIM)
    kern = functools.partial(_prep_kernel, n_chunks=bw // LANE, norm=gain is not None,
                             rope=rope is not None, scale=float(scale))
    return pl.pallas_call(
        kern,
        out_shape=jax.ShapeDtypeStruct((rows, width), BF16),
        grid=(rows // tr, width // bw),
        in_specs=[pl.BlockSpec((tr, bw), lambda i, j: (i, cb0 + j)),
                  pl.BlockSpec((tr, LANE), tab_map),
                  pl.BlockSpec((tr, LANE), tab_map),
                  pl.BlockSpec((1, HEAD_DIM), lambda i, j: (0, 0))],
        out_specs=pl.BlockSpec((tr, bw), lambda i, j: (i, j)),
        compiler_params=_cparams(("parallel", "parallel")),
        name="qk_prep",
    )(p, cos, sin, g)


def _rope_tables(seq, head_dim):
    t = jnp.arange(seq)
    rows = (t // GRID_W).astype(F32)
    cols = (t % GRID_W).astype(F32)
    quarter = head_dim // 4
    inv = ROPE_THETA ** (-jnp.arange(quarter, dtype=F32) / quarter)
    ang = jnp.concatenate([rows[:, None] * inv, cols[:, None] * inv], axis=-1)
    lane = np.arange(LANE)
    pair = (lane % head_dim) // 2
    sign = np.where(lane % 2 == 0, -1.0, 1.0).astype(np.float32)
    return jnp.cos(ang)[:, pair], jnp.sin(ang)[:, pair] * sign


def _qk(q, k):
    return lax.dot_general(q, k, (((1,), (1,)), ((), ())), preferred_element_type=F32)


def _stack_heads(q, n):
    if n == 1:
        return q
    return jnp.concatenate([q[:, g * LANE:(g + 1) * LANE] for g in range(n)], axis=0)


def _unstack_heads(o, n):
    if n == 1:
        return o
    t = o.shape[0] // n
    return jnp.concatenate([o[g * t:(g + 1) * t] for g in range(n)], axis=1)


def _stack_diff(q):
    lane = lax.broadcasted_iota(jnp.int32, q.shape, 1)
    zero = jnp.zeros_like(q)
    return jnp.concatenate([jnp.where(lane < DH_B, q, zero),
                            jnp.where(lane >= DH_B, q, zero)], axis=0)


def _softmax_attend(groups, sink=None):
    m = None
    for s, _ in groups:
        sm = jnp.max(s, axis=-1, keepdims=True)
        m = sm if m is None else jnp.maximum(m, sm)
    if sink is not None:
        m = jnp.maximum(m, sink)
    l = None
    acc = None
    for s, v in groups:
        p = jnp.exp(s - m)
        ps = jnp.sum(p, axis=-1, keepdims=True)
        pv = jnp.dot(p.astype(v.dtype), v, preferred_element_type=F32)
        l = ps if l is None else l + ps
        acc = pv if acc is None else acc + pv
    if sink is not None:
        l = l + jnp.exp(sink - m)
    return acc / l


def _diff_lambda(lam_ref, lam_init):
    a = jnp.sum(lam_ref[0:1, :] * lam_ref[1:2, :], axis=-1, keepdims=True)
    b = jnp.sum(lam_ref[2:3, :] * lam_ref[3:4, :], axis=-1, keepdims=True)
    return jnp.exp(a) - jnp.exp(b) + lam_init


def _diff_finalize(o, lam_ref, gsub_ref, lam_init):
    t = o.shape[0] // 2
    d = o[:t] - _diff_lambda(lam_ref, lam_init) * o[t:]
    return _rms(d, gsub_ref[...]) * (1.0 - lam_init)


def _sink_rows(sink_ref, kv, rows_per_head, n_rows):
    head = lax.broadcasted_iota(jnp.int32, (n_rows, 1), 0) // rows_per_head
    out = jnp.zeros((n_rows, 1), F32)
    for g in range(GQA_G):
        out = jnp.where(head == g, sink_ref[kv * GQA_G + g], out)
    return out


def _lane_tiles(s):
    return [s[:, j * LANE:(j + 1) * LANE] for j in range(s.shape[1] // LANE)]


def _flash_kernel(q_ref, kl_ref, vl_ref, kc_ref, vc_ref, lam_ref, gsub_ref, o_ref,
                  m_ref, l_ref, acc_ref, *, diff, n_stack, tk, lam_init):
    q = q_ref[...]
    q = _stack_diff(q) if diff else _stack_heads(q, n_stack)
    n_chunks = kl_ref.shape[0] // tk

    def lat(ref, c):
        return ref[pl.ds(pl.multiple_of(c * tk, tk), tk), :]

    def lane_max(s, m):
        for t in _lane_tiles(s):
            m = jnp.maximum(m, t)
        return m

    def max_body(c, carry):
        m_ref[...] = lane_max(_qk(q, lat(kl_ref, c)), m_ref[...])
        return carry

    m_ref[...] = jnp.full(m_ref.shape, -jnp.inf, F32)
    lax.fori_loop(0, n_chunks, max_body, 0, unroll=2)
    m_lane = lane_max(_qk(q, kc_ref[...]), m_ref[...])
    m_ref[...] = jnp.broadcast_to(jnp.max(m_lane, axis=-1, keepdims=True), m_ref.shape)

    def attend(k, v):
        m = m_ref[...]
        ps = [jnp.exp2(t - m) for t in _lane_tiles(_qk(q, k))]
        l = l_ref[...]
        for p in ps:
            l = l + p
        l_ref[...] = l
        p = jnp.concatenate(ps, axis=1).astype(v.dtype)
        acc_ref[...] += jnp.dot(p, v, preferred_element_type=F32)

    def sum_body(c, carry):
        attend(lat(kl_ref, c), lat(vl_ref, c))
        return carry

    l_ref[...] = jnp.zeros(l_ref.shape, F32)
    acc_ref[...] = jnp.zeros(acc_ref.shape, F32)
    lax.fori_loop(0, n_chunks, sum_body, 0, unroll=2)
    attend(kc_ref[...], vc_ref[...])
    o = acc_ref[...] / jnp.sum(l_ref[...], axis=-1, keepdims=True)
    if diff:
        o = _diff_finalize(o, lam_ref, gsub_ref, lam_init)
    else:
        o = _unstack_heads(o, n_stack)
    o_ref[...] = o.astype(o_ref.dtype)


def _flash_call(q, kl, kl_col, vl, vl_col, kc, kc_col, vc, vc_col, *, batch, seq, ctx,
                diff, lam_rows=None, g_sub=None, lam_init=0.0):
    n_heads = N_GROUP_HEADS if diff else KV_GQA
    n_stack = 2 if diff else GQA_G
    tq = min(512 // n_stack, seq)
    qw = LANE if diff else GQA_G * LANE
    nq = seq // tq
    rows = n_stack * tq
    if lam_rows is None:
        lam_rows = jnp.zeros((8, LANE), F32)
        g_sub = jnp.ones((HEAD_DIM,), F32)
    kern = functools.partial(_flash_kernel, diff=diff, n_stack=n_stack, tk=512,
                             lam_init=float(lam_init))
    return pl.pallas_call(
        kern,
        out_shape=jax.ShapeDtypeStruct((batch * seq, GROUP_W), BF16),
        grid=(batch, n_heads, nq),
        in_specs=[pl.BlockSpec((tq, qw), lambda b, h, i: (b * nq + i, h)),
                  pl.BlockSpec((seq, LANE), lambda b, h, i: (b, kl_col + h)),
                  pl.BlockSpec((seq, LANE), lambda b, h, i: (b, vl_col + h)),
                  pl.BlockSpec((ctx, LANE), lambda b, h, i: (b, kc_col + h)),
                  pl.BlockSpec((ctx, LANE), lambda b, h, i: (b, vc_col + h)),
                  pl.BlockSpec((8, LANE), lambda b, h, i: (0, 0)),
                  pl.BlockSpec((1, HEAD_DIM), lambda b, h, i: (0, 0))],
        out_specs=pl.BlockSpec((tq, qw), lambda b, h, i: (b * nq + i, h)),
        scratch_shapes=[pltpu.VMEM((rows, LANE), F32), pltpu.VMEM((rows, LANE), F32),
                        pltpu.VMEM((rows, HEAD_DIM), F32)],
        compiler_params=_cparams(("parallel", "parallel", "parallel")),
        name="attn_full",
    )(q, kl, vl, kc, vc, lam_rows, g_sub.reshape(1, HEAD_DIM))


def _band_kernel(sink_ref, q_ref, kp_ref, kx_ref, kn_ref, vp_ref, vx_ref, vn_ref,
                 kc_ref, vc_ref, o_ref, *, tq, seq):
    kv = pl.program_id(1)
    i = pl.program_id(2)
    q = _stack_heads(q_ref[...], GQA_G)
    rows = q.shape[0]
    k_loc = jnp.concatenate([kp_ref[...], kx_ref[...], kn_ref[...]], axis=0)
    v_loc = jnp.concatenate([vp_ref[...], vx_ref[...], vn_ref[...]], axis=0)
    s_loc = _qk(q, k_loc)
    qpos = i * tq + (lax.broadcasted_iota(jnp.int32, s_loc.shape, 0) & (tq - 1))
    kpos = (i - 1) * tq + lax.broadcasted_iota(jnp.int32, s_loc.shape, 1)
    ok = jnp.where(kpos >= 0, jnp.abs(qpos - kpos), WIN_C + 1)
    ok = jnp.where(kpos < seq, ok, WIN_C + 1) <= WIN_C
    s_loc = jnp.where(ok, s_loc, NEG_INF)
    s_ctx = _qk(q, kc_ref[...])
    sink = _sink_rows(sink_ref, kv, tq, rows)
    o = _softmax_attend([(s_loc, v_loc), (s_ctx, vc_ref[...])], sink)
    o_ref[...] = _unstack_heads(o, GQA_G).astype(o_ref.dtype)


def _band_call(sink, q, kl, p, pc, pc_off, *, batch, seq, ctx):
    tq = WIN_C
    nq = seq // tq
    qw = GQA_G * LANE

    def nb(delta, col):
        def imap(b, h, i):
            return (b * nq + jnp.clip(i + delta, 0, nq - 1), col + h)
        return imap

    return pl.pallas_call(
        functools.partial(_band_kernel, tq=tq, seq=seq),
        out_shape=jax.ShapeDtypeStruct((batch * seq, GROUP_W), BF16),
        grid=(batch, KV_GQA, nq),
        in_specs=[pl.BlockSpec(memory_space=pltpu.SMEM),
                  pl.BlockSpec((tq, qw), lambda b, h, i: (b * nq + i, h)),
                  pl.BlockSpec((tq, LANE), nb(-1, 0)),
                  pl.BlockSpec((tq, LANE), nb(0, 0)),
                  pl.BlockSpec((tq, LANE), nb(1, 0)),
                  pl.BlockSpec((tq, LANE), nb(-1, VC_COL)),
                  pl.BlockSpec((tq, LANE), nb(0, VC_COL)),
                  pl.BlockSpec((tq, LANE), nb(1, VC_COL)),
                  pl.BlockSpec((ctx, LANE), lambda b, h, i: (b, KC_COL - pc_off + h)),
                  pl.BlockSpec((ctx, LANE), lambda b, h, i: (b, VC_COL - pc_off + h))],
        out_specs=pl.BlockSpec((tq, qw), lambda b, h, i: (b * nq + i, h)),
        compiler_params=_cparams(("parallel", "parallel", "parallel")),
        name="attn_band",
    )(sink, q, kl, kl, kl, p, p, p, pc, pc)


def _nbr_build_bias(tab_ref, bias_ref):
    shape = (GRID_W, LANE)
    w = lax.broadcasted_iota(jnp.int32, shape, 0)
    lane = lax.broadcasted_iota(jnp.int32, shape, 1)
    c = lane & (GRID_W - 1)
    cs = jnp.clip(w - NA_KW // 2, 0, GRID_W - NA_KW)
    d = c - cs
    mask = jnp.where(d >= 0, jnp.where(d < NA_KW, 0.0, NEG_INF), NEG_INF).astype(F32)
    toep = [pltpu.roll(jnp.broadcast_to(tab_ref[ro:ro + 1, :], shape), LANE - (NA_KW - 1), 1,
                       stride=1, stride_axis=0) for ro in range(2 * NA_KH - 1)]
    for v in range(NA_KH):
        for j in range(NA_KH // 2):
            tile = jnp.where(lane < GRID_W, toep[v + 2 * j], toep[v + 2 * j + 1]) + mask
            bias_ref[v, :, j * LANE:(j + 1) * LANE] = tile


def _nbr_kernel(q_ref, k_ref, v_ref, kc_ref, vc_ref, tab_ref, o_ref, bias_ref, *, n_rows, scale):
    slab = NA_KH * GRID_W
    kc = kc_ref[...]
    vc = vc_ref[...]
    _nbr_build_bias(tab_ref, bias_ref)

    def body(r, carry):
        rs = jnp.clip(r - NA_KH // 2, 0, n_rows - NA_KH)
        var = rs - r + NA_KH - 1
        q = q_ref[pl.ds(pl.multiple_of(r * GRID_W, GRID_W), GRID_W), :]
        off = pl.multiple_of(rs * GRID_W, GRID_W)
        ks = k_ref[pl.ds(off, slab), :]
        vs = v_ref[pl.ds(off, slab), :]
        s_loc = _qk(q, ks) * scale + bias_ref[var]
        s_ctx = _qk(q, kc) * scale
        o = _softmax_attend([(s_loc, vs), (s_ctx, vc)])
        o_ref[pl.ds(pl.multiple_of(r * GRID_W, GRID_W), GRID_W), :] = o.astype(o_ref.dtype)
        return carry

    lax.fori_loop(0, n_rows, body, 0, unroll=2)


def _nbr_table(na_bias):
    h, nr, nc = na_bias.shape
    t = jnp.pad(na_bias.astype(F32), ((0, 0), (0, 16 - nr), (0, GRID_W - nc)))
    return jnp.concatenate([t] * (LANE // GRID_W), axis=-1)


def _nbr_call(p, pc, pc_off, na_bias, *, batch, seq, ctx):
    n_rows = seq // GRID_W
    assert n_rows >= NA_KH and 2 * GRID_W == LANE
    scale = HEAD_DIM ** -0.5
    return pl.pallas_call(
        functools.partial(_nbr_kernel, n_rows=n_rows, scale=scale),
        out_shape=jax.ShapeDtypeStruct((batch * seq, GROUP_W), BF16),
        grid=(batch, N_GROUP_HEADS),
        in_specs=[pl.BlockSpec((seq, LANE), lambda b, h: (b, QD_COL + h)),
                  pl.BlockSpec((seq, LANE), lambda b, h: (b, KD_COL + h)),
                  pl.BlockSpec((seq, LANE), lambda b, h: (b, VD_COL + h)),
                  pl.BlockSpec((ctx, LANE), lambda b, h: (b, KD_COL - pc_off + h)),
                  pl.BlockSpec((ctx, LANE), lambda b, h: (b, VD_COL - pc_off + h)),
                  pl.BlockSpec((None, 16, LANE), lambda b, h: (h, 0, 0))],
        out_specs=pl.BlockSpec((seq, LANE), lambda b, h: (b, h)),
        scratch_shapes=[pltpu.VMEM((NA_KH, GRID_W, NA_KH * GRID_W), F32)],
        compiler_params=_cparams(("parallel", "parallel")),
        name="attn_nbr",
    )(p, p, p, pc, pc, _nbr_table(na_bias))


def _ctx_kernel(sink_ref, q_ref, k_ref, v_ref, lam_ref, gsub_ref, o_ref, *,
                diff, n_stack, scale, use_sink, lam_init):
    q = q_ref[...]
    q = _stack_diff(q) if diff else _stack_heads(q, n_stack)
    s = _qk(q, k_ref[...])
    if scale != 1.0:
        s = s * scale
    sink = None
    if use_sink:
        sink = _sink_rows(sink_ref, pl.program_id(1), q_ref.shape[0], q.shape[0])
    o = _softmax_attend([(s, v_ref[...])], sink)
    if diff:
        o = _diff_finalize(o, lam_ref, gsub_ref, lam_init)
    else:
        o = _unstack_heads(o, n_stack)
    o_ref[...] = o.astype(o_ref.dtype)


def _ctx_call(q, q_col, k, k_col, v, v_col, *, batch, ctx, n_heads, n_stack, diff=False,
              scale=1.0, sink=None, lam_rows=None, g_sub=None, lam_init=0.0):
    qw = LANE if diff else n_stack * LANE
    qb = q_col * LANE // qw
    if lam_rows is None:
        lam_rows = jnp.zeros((8, LANE), F32)
        g_sub = jnp.ones((HEAD_DIM,), F32)
    use_sink = sink is not None
    if sink is None:
        sink = jnp.zeros((N_GROUP_HEADS,), F32)
    kern = functools.partial(_ctx_kernel, diff=diff, n_stack=n_stack, scale=float(scale),
                             use_sink=use_sink, lam_init=float(lam_init))
    return pl.pallas_call(
        kern,
        out_shape=jax.ShapeDtypeStruct((batch * ctx, GROUP_W), BF16),
        grid=(batch, n_heads),
        in_specs=[pl.BlockSpec(memory_space=pltpu.SMEM),
                  pl.BlockSpec((ctx, qw), lambda b, h: (b, qb + h)),
                  pl.BlockSpec((ctx, LANE), lambda b, h: (b, k_col + h)),
                  pl.BlockSpec((ctx, LANE), lambda b, h: (b, v_col + h)),
                  pl.BlockSpec((8, LANE), lambda b, h: (0, 0)),
                  pl.BlockSpec((1, HEAD_DIM), lambda b, h: (0, 0))],
        out_specs=pl.BlockSpec((ctx, qw), lambda b, h: (b, h)),
        compiler_params=_cparams(("parallel", "parallel")),
        name="attn_ctx",
    )(sink, q, k, v, lam_rows, g_sub.reshape(1, HEAD_DIM))


def _ffn_up_kernel(a_ref, ap_ref, an_ref, wg_ref, wv_ref, cwg_ref, cwv_ref, o_ref, ext_ref, *,
                   tiles_per_seq):
    i = pl.program_id(0)
    tm = a_ref.shape[0]

    @pl.when(pl.program_id(1) == 0)
    def _():
        first = (i % tiles_per_seq) == 0
        last = (i % tiles_per_seq) == tiles_per_seq - 1
        zero = jnp.zeros(ap_ref.shape, ap_ref.dtype)
        ext_ref[0:HALO, :] = jnp.where(first, zero, ap_ref[...])
        ext_ref[HALO:HALO + tm, :] = a_ref[...]
        ext_ref[HALO + tm:, :] = jnp.where(last, zero, an_ref[...])

    a = ext_ref[...]

    def conv(w_ref, cw_ref):
        u = jnp.dot(a, w_ref[...].astype(BF16), preferred_element_type=F32)
        n = u.shape[0]
        cu = (pltpu.roll(u, 1, 0) * cw_ref[0:1, :] + u * cw_ref[1:2, :]
              + pltpu.roll(u, n - 1, 0) * cw_ref[2:3, :])
        return cu[HALO:HALO + tm]

    gate = conv(wg_ref, cwg_ref)
    val = conv(wv_ref, cwv_ref)
    o_ref[...] = (gate * jax.nn.sigmoid(gate) * val).astype(o_ref.dtype)


def _ffn_up_call(a, w_up, conv_w, seq):
    m, kdim = a.shape
    f = w_up.shape[1] // 2
    tm = min(1024, seq)
    tn = _pick(f, (256, 128))
    tiles_per_seq = seq // tm
    fb = f // tn
    hb = tm // HALO
    n_halo = m // HALO
    cw = jnp.pad(conv_w, ((0, 8 - CONV_W), (0, 0)))
    return pl.pallas_call(
        functools.partial(_ffn_up_kernel, tiles_per_seq=tiles_per_seq),
        out_shape=jax.ShapeDtypeStruct((m, f), BF16),
        grid=(m // tm, fb),
        in_specs=[pl.BlockSpec((tm, kdim), lambda i, j: (i, 0)),
                  pl.BlockSpec((HALO, kdim), lambda i, j: (jnp.maximum(i * hb - 1, 0), 0)),
                  pl.BlockSpec((HALO, kdim),
                               lambda i, j: (jnp.minimum((i + 1) * hb, n_halo - 1), 0)),
                  pl.BlockSpec((kdim, tn), lambda i, j: (0, j)),
                  pl.BlockSpec((kdim, tn), lambda i, j: (0, fb + j)),
                  pl.BlockSpec((8, tn), lambda i, j: (0, j)),
                  pl.BlockSpec((8, tn), lambda i, j: (0, fb + j))],
        out_specs=pl.BlockSpec((tm, tn), lambda i, j: (i, j)),
        scratch_shapes=[pltpu.VMEM((tm + 2 * HALO, kdim), BF16)],
        compiler_params=_cparams(("parallel", "arbitrary")),
        name="ffn_up",
    )(a, a, a, w_up, w_up, cw, cw)


def _conv_ffn(hx2d, w_up, conv_w, w_down, seq):
    g = _ffn_up_call(hx2d, w_up, conv_w, seq)
    return _matmul_call(g, w_down, F32)


def kernel(x, c, ctx, c_ctx, w_ada, b_ada, g_attn_pre, g_attn_post, g_mlp_pre, g_mlp_post,
           w_in, w_out, g_q_a, g_k_a, lam_q1, lam_k1, lam_q2, lam_k2, g_sub_b, sink_c, na_bias,
           w_up, conv_ffn_w, w_down):
    batch, seq, d = x.shape
    n_ctx = ctx.shape[1]
    depth = w_ada.shape[0]
    assert batch + 1 <= 8 and d == N_Q

    cond = jnp.zeros((8, d), F32).at[:batch].set(c).at[batch].set(c_ctx)
    mod_all = _ada_call(cond, w_ada, b_ada)
    rope_hd = _rope_tables(seq, HEAD_DIM)
    rope_dh = _rope_tables(seq, DH_B)
    scale_hd = HEAD_DIM ** -0.5
    scale_dh = DH_B ** -0.5

    h = ctx
    for l in range(depth):
        last = l == depth - 1
        lam_init = 0.8 - 0.6 * math.exp(-0.3 * l)
        mod = mod_all[l, :batch].reshape(batch, 6, d)
        sh_a, sc_a, gt_a, sh_m, sc_m, gt_m = (mod[:, k] for k in range(6))
        mod_c = mod_all[l, batch].reshape(6, d)
        one = lambda v: v.reshape(1, d)
        wi, wo, wu = w_in[l], w_out[l], w_up[l]
        wd = w_down[l].astype(BF16)
        lam_rows = jnp.zeros((8, LANE), F32).at[:4, :DH_B].set(
            jnp.stack([lam_q1[l], lam_k1[l], lam_q2[l], lam_k2[l]]))

        hx = _norm_mod_call(x, g_attn_pre[l], sc_a, sh_a).reshape(batch * seq, d)
        hc = _norm_mod_call(h.reshape(1, batch * n_ctx, d), g_attn_pre[l],
                            one(mod_c[1]), one(mod_c[0])).reshape(batch * n_ctx, d)
        p = _matmul_ws_call(hx, wi, BF16)
        pc_off = 32 if last else 0
        pc = _matmul_ws_call(hc, wi, BF16, col0=pc_off * LANE)

        q_a = _prep_call(p, QA_COL, GROUP_W, seq, gain=g_q_a[l], rope=rope_hd,
                         scale=scale_hd * LOG2E)
        k_a = _prep_call(p, KA_COL, KV_GQA * LANE, seq, gain=g_k_a[l], rope=rope_hd)
        kc_a = _prep_call(pc, KA_COL - pc_off, KV_GQA * LANE, n_ctx, gain=g_k_a[l])
        q_b = _prep_call(p, QB_COL, GROUP_W, seq, rope=rope_dh, scale=scale_dh * LOG2E)
        k_b = _prep_call(p, KB_COL, GROUP_W, seq, rope=rope_dh)
        q_c = _prep_call(p, QC_COL, GROUP_W, seq, rope=rope_hd, scale=scale_hd)
        k_c = _prep_call(p, KC_COL, KV_GQA * LANE, seq, rope=rope_hd)

        dims = dict(batch=batch, seq=seq, ctx=n_ctx)
        o_a = _flash_call(q_a, k_a, 0, p, VA_COL, kc_a, 0, pc, VA_COL - pc_off, diff=False, **dims)
        o_b = _flash_call(q_b, k_b, 0, p, VB_COL, pc, KB_COL - pc_off, pc, VB_COL - pc_off,
                          diff=True, lam_rows=lam_rows, g_sub=g_sub_b[l], lam_init=lam_init, **dims)
        o_c = _band_call(sink_c[l], q_c, k_c, p, pc, pc_off, **dims)
        o_d = _nbr_call(p, pc, pc_off, na_bias[l], **dims)
        o = jnp.concatenate([o_a, o_b, o_c, o_d], axis=-1)
        y = _matmul_ws_call(o, wo, F32)
        x = _norm_res_call(x, y, g_attn_post[l], gt_a)

        hx = _norm_mod_call(x, g_mlp_pre[l], sc_m, sh_m).reshape(batch * seq, d)
        y = _conv_ffn(hx, wu, conv_ffn_w[l], wd, seq)
        x = _norm_res_call(x, y, g_mlp_post[l], gt_m)

        if not last:
            cdims = dict(batch=batch, ctx=n_ctx)
            qc_a = _prep_call(pc, QA_COL, GROUP_W, n_ctx, gain=g_q_a[l], scale=scale_hd)
            co_a = _ctx_call(qc_a, 0, kc_a, 0, pc, VA_COL, n_heads=KV_GQA, n_stack=GQA_G, **cdims)
            co_b = _ctx_call(pc, QB_COL, pc, KB_COL, pc, VB_COL, n_heads=N_GROUP_HEADS,
                             n_stack=2, diff=True, scale=scale_dh, lam_rows=lam_rows,
                             g_sub=g_sub_b[l], lam_init=lam_init, **cdims)
            co_c = _ctx_call(pc, QC_COL, pc, KC_COL, pc, VC_COL, n_heads=KV_GQA, n_stack=GQA_G,
                             scale=scale_hd, sink=sink_c[l], **cdims)
            co_d = _ctx_call(pc, QD_COL, pc, KD_COL, pc, VD_COL, n_heads=N_GROUP_HEADS,
                             n_stack=1, scale=scale_hd, **cdims)
            co = jnp.concatenate([co_a, co_b, co_c, co_d], axis=-1)
            yc = _matmul_ws_call(co, wo, F32)
            h3 = h.reshape(1, batch * n_ctx, d)
            h3 = _norm_res_call(h3, yc, g_attn_post[l], one(mod_c[2]))
            hc = _norm_mod_call(h3, g_mlp_pre[l], one(mod_c[4]), one(mod_c[3]))
            yc = _conv_ffn(hc.reshape(batch * n_ctx, d), wu, conv_ffn_w[l], wd, n_ctx)
            h = _norm_res_call(h3, yc, g_mlp_post[l], one(mod_c[5])).reshape(batch, n_ctx, d)
    return x
```

```python
import functools
import math

import jax
import jax.numpy as jnp
import numpy as np
from jax import lax
from jax.experimental import pallas as pl
from jax.experimental.pallas import tpu as pltpu

F32 = jnp.float32
BF16 = jnp.bfloat16

GRID_W = 64
HEAD_DIM = 128
N_GROUP_HEADS = 8
KV_GQA = 2
GQA_G = N_GROUP_HEADS // KV_GQA
DH_B = HEAD_DIM // 2
WIN_C = 128
NA_KH = 8
NA_KW = 16
NBR_GROUP = 8
NBR_SLAB = 16
CONV_W = 3
ROPE_THETA = 10000.0
EPS = 1e-6
NEG_INF = -1e30
LOG2E = math.log2(math.e)
GROUP_W = N_GROUP_HEADS * HEAD_DIM
N_Q = 4 * GROUP_W

QA_COL, QB_COL, QC_COL, QD_COL = 0, 8, 16, 24
KA_COL, VA_COL = 32, 34
KB_COL, VB_COL = 36, 44
KC_COL, VC_COL = 52, 54
KD_COL, VD_COL = 56, 64
KV_COLS = 40

LANE = 128
HALO = 16
EPI_ROWS = 128
VMEM_LIMIT = 56 * 1024 * 1024


def _cparams(sem):
    return pltpu.CompilerParams(dimension_semantics=sem, vmem_limit_bytes=VMEM_LIMIT)


def _ada_kernel(a_ref, w_ref, b_ref, o_ref):
    a = a_ref[...]
    a = a * jax.nn.sigmoid(a)
    o_ref[...] = jnp.dot(a.astype(BF16), w_ref[...].astype(BF16),
                         preferred_element_type=F32) + b_ref[...]


def _ada_call(cond, w_ada, b_ada):
    n_layers, d, n = w_ada.shape
    tn = 512
    return pl.pallas_call(
        _ada_kernel,
        out_shape=jax.ShapeDtypeStruct((n_layers, 8, n), F32),
        grid=(n_layers, n // tn),
        in_specs=[pl.BlockSpec((8, d), lambda l, j: (0, 0)),
                  pl.BlockSpec((None, d, tn), lambda l, j: (l, 0, j)),
                  pl.BlockSpec((None, 1, tn), lambda l, j: (l, 0, j))],
        out_specs=pl.BlockSpec((None, 8, tn), lambda l, j: (l, 0, j)),
        compiler_params=_cparams(("parallel", "parallel")),
        name="ada_mod",
    )(cond, w_ada, b_ada.reshape(n_layers, 1, n))


def _rms(x, g):
    return x * lax.rsqrt(jnp.mean(x * x, axis=-1, keepdims=True) + EPS) * g


def _norm_mod_kernel(x_ref, g_ref, sc_ref, sh_ref, o_ref):
    y = _rms(x_ref[...], g_ref[...])
    o_ref[...] = (y * (1.0 + sc_ref[...]) + sh_ref[...]).astype(o_ref.dtype)


def _norm_mod_call(x, g, sc, sh):
    nb, r, d = x.shape
    tr = min(256, r)
    return pl.pallas_call(
        _norm_mod_kernel,
        out_shape=jax.ShapeDtypeStruct((nb, r, d), BF16),
        grid=(nb, r // tr),
        in_specs=[pl.BlockSpec((None, tr, d), lambda b, i: (b, i, 0)),
                  pl.BlockSpec((1, d), lambda b, i: (0, 0)),
                  pl.BlockSpec((None, 1, d), lambda b, i: (b, 0, 0)),
                  pl.BlockSpec((None, 1, d), lambda b, i: (b, 0, 0))],
        out_specs=pl.BlockSpec((None, tr, d), lambda b, i: (b, i, 0)),
        compiler_params=_cparams(("parallel", "parallel")),
        name="norm_mod",
    )(x, g.reshape(1, d), sc.reshape(nb, 1, d), sh.reshape(nb, 1, d))


def _norm_res_kernel(x_ref, y_ref, g_ref, gt_ref, o_ref):
    o_ref[...] = x_ref[...] + gt_ref[...] * _rms(y_ref[...], g_ref[...])


def _norm_res_call(x, y, g, gate):
    nb, r, d = x.shape
    tr = min(256, r)
    return pl.pallas_call(
        _norm_res_kernel,
        out_shape=jax.ShapeDtypeStruct((nb, r, d), F32),
        grid=(nb, r // tr),
        in_specs=[pl.BlockSpec((None, tr, d), lambda b, i: (b, i, 0)),
                  pl.BlockSpec((None, tr, d), lambda b, i: (b, i, 0)),
                  pl.BlockSpec((1, d), lambda b, i: (0, 0)),
                  pl.BlockSpec((None, 1, d), lambda b, i: (b, 0, 0))],
        out_specs=pl.BlockSpec((None, tr, d), lambda b, i: (b, i, 0)),
        compiler_params=_cparams(("parallel", "parallel")),
        name="norm_res",
    )(x, y.reshape(nb, r, d), g.reshape(1, d), gate.reshape(nb, 1, d))


def _matmul_kernel(a_ref, w_ref, o_ref, acc_ref, *, nk):
    k = pl.program_id(2)
    part = jnp.dot(a_ref[...], w_ref[...], preferred_element_type=F32)
    if nk == 1:
        o_ref[...] = part.astype(o_ref.dtype)
    else:
        @pl.when(k == 0)
        def _():
            acc_ref[...] = part

        @pl.when(jnp.logical_and(k > 0, k < nk - 1))
        def _():
            acc_ref[...] += part

        @pl.when(k == nk - 1)
        def _():
            o_ref[...] = (acc_ref[...] + part).astype(o_ref.dtype)


def _pick(n, prefs):
    for t in prefs:
        if n % t == 0:
            return t
    return n


def _matmul_call(a, w, out_dtype):
    m, kdim = a.shape
    _, n = w.shape
    tm = _pick(m, (1024, 512, 256))
    tn = _pick(n, (512, 256, 128))
    tk = kdim if kdim <= 4096 else _pick(kdim, (5504, 4096, 2048))
    nk = kdim // tk
    return pl.pallas_call(
        functools.partial(_matmul_kernel, nk=nk),
        out_shape=jax.ShapeDtypeStruct((m, n), out_dtype),
        grid=(m // tm, n // tn, nk),
        in_specs=[pl.BlockSpec((tm, tk), lambda i, j, k: (i, k)),
                  pl.BlockSpec((tk, tn), lambda i, j, k: (k, j))],
        out_specs=pl.BlockSpec((tm, tn), lambda i, j, k: (i, j)),
        scratch_shapes=[pltpu.VMEM((tm, tn) if nk > 1 else (8, LANE), F32)],
        compiler_params=_cparams(("parallel", "parallel", "arbitrary")),
        name="matmul",
    )(a, w)


def _matmul_ws_kernel(a_ref, w_ref, o_ref, wb_ref):
    @pl.when(pl.program_id(1) == 0)
    def _():
        wb_ref[...] = w_ref[...].astype(BF16)

    o_ref[...] = jnp.dot(a_ref[...], wb_ref[...], preferred_element_type=F32).astype(o_ref.dtype)


def _matmul_ws_call(a, w, layer, out_dtype, col0=0):
    m, kdim = a.shape
    n = w.shape[2] - col0
    tm = _pick(m, (1024, 512, 256))
    tn = _pick(n, (512, 256, 128))
    assert col0 % tn == 0
    cb0 = col0 // tn
    return pl.pallas_call(
        _matmul_ws_kernel,
        out_shape=jax.ShapeDtypeStruct((m, n), out_dtype),
        grid=(n // tn, m // tm),
        in_specs=[pl.BlockSpec((tm, kdim), lambda j, i: (i, 0)),
                  pl.BlockSpec((None, kdim, tn), lambda j, i: (layer, 0, cb0 + j))],
        out_specs=pl.BlockSpec((tm, tn), lambda j, i: (i, j)),
        scratch_shapes=[pltpu.VMEM((kdim, tn), BF16)],
        compiler_params=_cparams(("parallel", "arbitrary")),
        name="matmul_ws",
    )(a, w)


def _swap_pairs(y):
    lane = lax.broadcasted_iota(jnp.int32, y.shape, 1)
    nxt = pltpu.roll(y, LANE - 1, 1)
    prv = pltpu.roll(y, 1, 1)
    return jnp.where((lane & 1) == 0, nxt, prv)


def _prep_kernel(p_ref, cos_ref, sin_ref, g_ref, o_ref, *, n_chunks, norm, rope, scale):
    for c in range(n_chunks):
        sl = slice(c * LANE, (c + 1) * LANE)
        y = p_ref[:, sl].astype(F32)
        if norm:
            y = _rms(y, g_ref[...])
        if rope:
            y = y * cos_ref[...] + _swap_pairs(y) * sin_ref[...]
        if scale != 1.0:
            y = y * scale
        o_ref[:, sl] = y.astype(o_ref.dtype)


def _prep_call(p, col0, width, seq, *, gain=None, rope=None, scale=1.0):
    rows = p.shape[0]
    tr = min(256, seq)
    n_seq_tiles = seq // tr
    bw = min(width, 512)
    assert (col0 * LANE) % bw == 0 and width % bw == 0
    cb0 = col0 * LANE // bw
    if rope is None:
        cos = sin = jnp.zeros((tr, LANE), F32)
        tab_map = lambda i, j: (0, 0)
    else:
        cos, sin = rope
        tab_map = lambda i, j: (i % n_seq_tiles, 0)
    g = (jnp.ones((HEAD_DIM,), F32) if gain is None else gain).reshape(1, HEAD_DIM)
    kern = functools.partial(_prep_kernel, n_chunks=bw // LANE, norm=gain is not None,
                             rope=rope is not None, scale=float(scale))
    return pl.pallas_call(
        kern,
        out_shape=jax.ShapeDtypeStruct((rows, width), BF16),
        grid=(rows // tr, width // bw),
        in_specs=[pl.BlockSpec((tr, bw), lambda i, j: (i, cb0 + j)),
                  pl.BlockSpec((tr, LANE), tab_map),
                  pl.BlockSpec((tr, LANE), tab_map),
                  pl.BlockSpec((1, HEAD_DIM), lambda i, j: (0, 0))],
        out_specs=pl.BlockSpec((tr, bw), lambda i, j: (i, j)),
        compiler_params=_cparams(("parallel", "parallel")),
        name="qk_prep",
    )(p, cos, sin, g)


def _rope_tables(seq, head_dim):
    t = jnp.arange(seq)
    rows = (t // GRID_W).astype(F32)
    cols = (t % GRID_W).astype(F32)
    quarter = head_dim // 4
    inv = ROPE_THETA ** (-jnp.arange(quarter, dtype=F32) / quarter)
    ang = jnp.concatenate([rows[:, None] * inv, cols[:, None] * inv], axis=-1)
    lane = np.arange(LANE)
    pair = (lane % head_dim) // 2
    sign = np.where(lane % 2 == 0, -1.0, 1.0).astype(np.float32)
    return jnp.cos(ang)[:, pair], jnp.sin(ang)[:, pair] * sign


def _qk(q, k):
    return lax.dot_general(q, k, (((1,), (1,)), ((), ())), preferred_element_type=F32)


def _stack_heads(q, n):
    if n == 1:
        return q
    return jnp.concatenate([q[:, g * LANE:(g + 1) * LANE] for g in range(n)], axis=0)


def _unstack_heads(o, n):
    if n == 1:
        return o
    t = o.shape[0] // n
    return jnp.concatenate([o[g * t:(g + 1) * t] for g in range(n)], axis=1)


def _stack_diff(q):
    lane = lax.broadcasted_iota(jnp.int32, q.shape, 1)
    zero = jnp.zeros_like(q)
    return jnp.concatenate([jnp.where(lane < DH_B, q, zero),
                            jnp.where(lane >= DH_B, q, zero)], axis=0)


def _softmax_attend(groups, sink=None):
    m = None
    for s, _ in groups:
        sm = jnp.max(s, axis=-1, keepdims=True)
        m = sm if m is None else jnp.maximum(m, sm)
    if sink is not None:
        m = jnp.maximum(m, sink)
    l = None
    acc = None
    for s, v in groups:
        p = jnp.exp(s - m)
        ps = jnp.sum(p, axis=-1, keepdims=True)
        pv = jnp.dot(p.astype(v.dtype), v, preferred_element_type=F32)
        l = ps if l is None else l + ps
        acc = pv if acc is None else acc + pv
    if sink is not None:
        l = l + jnp.exp(sink - m)
    return acc / l


def _diff_lambda(lam_ref, lam_init):
    a = jnp.sum(lam_ref[0:1, :] * lam_ref[1:2, :], axis=-1, keepdims=True)
    b = jnp.sum(lam_ref[2:3, :] * lam_ref[3:4, :], axis=-1, keepdims=True)
    return jnp.exp(a) - jnp.exp(b) + lam_init


def _diff_finalize(o, lam_ref, gsub_ref, lam_init):
    t = o.shape[0] // 2
    d = o[:t] - _diff_lambda(lam_ref, lam_init) * o[t:]
    return _rms(d, gsub_ref[...]) * (1.0 - lam_init)


def _sink_rows(sink_ref, kv, rows_per_head, n_rows):
    head = lax.broadcasted_iota(jnp.int32, (n_rows, 1), 0) // rows_per_head
    out = jnp.zeros((n_rows, 1), F32)
    for g in range(GQA_G):
        out = jnp.where(head == g, sink_ref[kv * GQA_G + g], out)
    return out


def _lane_tiles(s):
    return [s[:, j * LANE:(j + 1) * LANE] for j in range(s.shape[1] // LANE)]


def _flash_kernel(q_ref, kl_ref, vl_ref, kc_ref, vc_ref, lam_ref, gsub_ref, o_ref, *,
                  diff, n_stack, tk, lam_init):
    q = q_ref[...]
    q = _stack_diff(q) if diff else _stack_heads(q, n_stack)

    def row_max(tiles):
        m = tiles[0]
        for t in tiles[1:]:
            m = jnp.maximum(m, t)
        return jnp.broadcast_to(jnp.max(m, axis=-1, keepdims=True), m.shape)

    def weights(tiles, m, v):
        ps = [jnp.exp2(t - m) for t in tiles]
        l = ps[0]
        for p in ps[1:]:
            l = l + p
        pv = jnp.dot(jnp.concatenate(ps, axis=1).astype(v.dtype), v, preferred_element_type=F32)
        return l, pv

    def latent_scores(c):
        return _lane_tiles(_qk(q, kl_ref[c * tk:(c + 1) * tk, :]))

    n_chunks = kl_ref.shape[0] // tk
    tiles = _lane_tiles(_qk(q, kc_ref[...]))
    nxt = latent_scores(0)
    m = row_max(tiles)
    l, acc = weights(tiles, m, vc_ref[...])
    for c in range(n_chunks):
        tiles = nxt
        if c + 1 < n_chunks:
            nxt = latent_scores(c + 1)
        m_new = jnp.maximum(m, row_max(tiles))
        alpha = jnp.exp2(m - m_new)
        l_c, pv = weights(tiles, m_new, vl_ref[c * tk:(c + 1) * tk, :])
        l = alpha * l + l_c
        acc = alpha * acc + pv
        m = m_new
    o = acc / jnp.sum(l, axis=-1, keepdims=True)
    if diff:
        o = _diff_finalize(o, lam_ref, gsub_ref, lam_init)
    else:
        o = _unstack_heads(o, n_stack)
    o_ref[...] = o.astype(o_ref.dtype)


def _flash_call(q, kl, kl_col, vl, vl_col, kc, kc_col, vc, vc_col, *, batch, seq, ctx,
                diff, lam_rows=None, g_sub=None, lam_init=0.0):
    n_heads = N_GROUP_HEADS if diff else KV_GQA
    n_stack = 2 if diff else GQA_G
    tq = min(512 // n_stack, seq)
    qw = LANE if diff else GQA_G * LANE
    nq = seq // tq
    if lam_rows is None:
        lam_rows = jnp.zeros((8, LANE), F32)
        g_sub = jnp.ones((HEAD_DIM,), F32)
    kern = functools.partial(_flash_kernel, diff=diff, n_stack=n_stack, tk=512,
                             lam_init=float(lam_init))
    return pl.pallas_call(
        kern,
        out_shape=jax.ShapeDtypeStruct((batch * seq, GROUP_W), BF16),
        grid=(batch, n_heads, nq),
        in_specs=[pl.BlockSpec((tq, qw), lambda b, h, i: (b * nq + i, h)),
                  pl.BlockSpec((seq, LANE), lambda b, h, i: (b, kl_col + h)),
                  pl.BlockSpec((seq, LANE), lambda b, h, i: (b, vl_col + h)),
                  pl.BlockSpec((ctx, LANE), lambda b, h, i: (b, kc_col + h)),
                  pl.BlockSpec((ctx, LANE), lambda b, h, i: (b, vc_col + h)),
                  pl.BlockSpec((8, LANE), lambda b, h, i: (0, 0)),
                  pl.BlockSpec((1, HEAD_DIM), lambda b, h, i: (0, 0))],
        out_specs=pl.BlockSpec((tq, qw), lambda b, h, i: (b * nq + i, h)),
        compiler_params=_cparams(("parallel", "parallel", "parallel")),
        name="attn_full",
    )(q, kl, vl, kc, vc, lam_rows, g_sub.reshape(1, HEAD_DIM))


def _band_kernel(sink_ref, q_ref, kp_ref, kx_ref, kn_ref, vp_ref, vx_ref, vn_ref,
                 kc_ref, vc_ref, o_ref, *, tq, seq):
    kv = pl.program_id(1)
    i = pl.program_id(2)
    q = _stack_heads(q_ref[...], GQA_G)
    rows = q.shape[0]
    k_loc = jnp.concatenate([kp_ref[...], kx_ref[...], kn_ref[...]], axis=0)
    v_loc = jnp.concatenate([vp_ref[...], vx_ref[...], vn_ref[...]], axis=0)
    s_loc = _qk(q, k_loc)
    qpos = i * tq + (lax.broadcasted_iota(jnp.int32, s_loc.shape, 0) & (tq - 1))
    kpos = (i - 1) * tq + lax.broadcasted_iota(jnp.int32, s_loc.shape, 1)
    ok = jnp.where(kpos >= 0, jnp.abs(qpos - kpos), WIN_C + 1)
    ok = jnp.where(kpos < seq, ok, WIN_C + 1) <= WIN_C
    s_loc = jnp.where(ok, s_loc, NEG_INF)
    s_ctx = _qk(q, kc_ref[...])
    sink = _sink_rows(sink_ref, kv, tq, rows)
    o = _softmax_attend([(s_loc, v_loc), (s_ctx, vc_ref[...])], sink)
    o_ref[...] = _unstack_heads(o, GQA_G).astype(o_ref.dtype)


def _band_call(sink, q, kl, p, pc, pc_off, *, batch, seq, ctx):
    tq = WIN_C
    nq = seq // tq
    qw = GQA_G * LANE

    def nb(delta, col):
        def imap(b, h, i):
            return (b * nq + jnp.clip(i + delta, 0, nq - 1), col + h)
        return imap

    return pl.pallas_call(
        functools.partial(_band_kernel, tq=tq, seq=seq),
        out_shape=jax.ShapeDtypeStruct((batch * seq, GROUP_W), BF16),
        grid=(batch, KV_GQA, nq),
        in_specs=[pl.BlockSpec(memory_space=pltpu.SMEM),
                  pl.BlockSpec((tq, qw), lambda b, h, i: (b * nq + i, h)),
                  pl.BlockSpec((tq, LANE), nb(-1, 0)),
                  pl.BlockSpec((tq, LANE), nb(0, 0)),
                  pl.BlockSpec((tq, LANE), nb(1, 0)),
                  pl.BlockSpec((tq, LANE), nb(-1, VC_COL)),
                  pl.BlockSpec((tq, LANE), nb(0, VC_COL)),
                  pl.BlockSpec((tq, LANE), nb(1, VC_COL)),
                  pl.BlockSpec((ctx, LANE), lambda b, h, i: (b, KC_COL - pc_off + h)),
                  pl.BlockSpec((ctx, LANE), lambda b, h, i: (b, VC_COL - pc_off + h))],
        out_specs=pl.BlockSpec((tq, qw), lambda b, h, i: (b * nq + i, h)),
        compiler_params=_cparams(("parallel", "parallel", "parallel")),
        name="attn_band",
    )(sink, q, kl, kl, kl, p, p, p, pc, pc)


def _nbr_build_bias(tab_ref, bias_ref):
    shape = (GRID_W, LANE)
    w = lax.broadcasted_iota(jnp.int32, shape, 0)
    lane = lax.broadcasted_iota(jnp.int32, shape, 1)
    c = lane & (GRID_W - 1)
    cs = jnp.clip(w - NA_KW // 2, 0, GRID_W - NA_KW)
    d = c - cs
    mask = jnp.where(d >= 0, jnp.where(d < NA_KW, 0.0, NEG_INF), NEG_INF).astype(F32)
    neg = jnp.full(shape, NEG_INF, F32)
    toep = [pltpu.roll(jnp.broadcast_to(tab_ref[ro:ro + 1, :], shape), LANE - (NA_KW - 1), 1,
                       stride=1, stride_axis=0) + mask for ro in range(2 * NA_KH - 1)]
    kinds = (lambda dr: (dr, max(dr - NA_KH // 2, 0)),
             lambda dr: (NA_KH // 2 + dr, dr),
             lambda dr: (NBR_GROUP + dr, min(NA_KH // 2 + dr, NBR_SLAB - NA_KH)))
    for kind, rel in enumerate(kinds):
        for dr in range(NBR_GROUP):
            r_rel, rs_rel = rel(dr)
            for j in range(NBR_SLAB // 2):
                halves = [toep[kr - r_rel + NA_KH - 1] if rs_rel <= kr < rs_rel + NA_KH else neg
                          for kr in (2 * j, 2 * j + 1)]
                tile = neg if halves[0] is neg and halves[1] is neg else jnp.where(
                    lane < GRID_W, halves[0], halves[1])
                bias_ref[kind, dr * GRID_W:(dr + 1) * GRID_W, j * LANE:(j + 1) * LANE] = tile


def _nbr_kernel(q_ref, k_ref, v_ref, kc_ref, vc_ref, tab_ref, o_ref, bias_ref, *, n_rows, scale):
    n_groups = n_rows // NBR_GROUP
    gq = NBR_GROUP * GRID_W
    slab = NBR_SLAB * GRID_W
    kc = kc_ref[...]
    vc = vc_ref[...]
    _nbr_build_bias(tab_ref, bias_ref)

    def body(g, carry):
        kind = jnp.where(g == 0, 0, jnp.where(g == n_groups - 1, 2, 1))
        base = jnp.clip(g * NBR_GROUP - NA_KH // 2, 0, n_rows - NBR_SLAB)
        rows = pl.ds(pl.multiple_of(g * gq, gq), gq)
        keys = pl.ds(pl.multiple_of(base * GRID_W, GRID_W), slab)
        q = q_ref[rows, :]
        s_loc = _qk(q, k_ref[keys, :]) * scale + bias_ref[kind]
        s_ctx = _qk(q, kc) * scale
        o = _softmax_attend([(s_loc, v_ref[keys, :]), (s_ctx, vc)])
        o_ref[rows, :] = o.astype(o_ref.dtype)
        return carry

    lax.fori_loop(0, n_groups, body, 0)


def _nbr_table(na_bias):
    h, nr, nc = na_bias.shape
    t = jnp.pad(na_bias.astype(F32), ((0, 0), (0, 16 - nr), (0, GRID_W - nc)))
    return jnp.concatenate([t] * (LANE // GRID_W), axis=-1)


def _nbr_call(p, pc, pc_off, na_bias, *, batch, seq, ctx):
    n_rows = seq // GRID_W
    assert n_rows >= NBR_SLAB and n_rows % NBR_GROUP == 0 and 2 * GRID_W == LANE
    assert NBR_SLAB >= NBR_GROUP + NA_KH - 1 and NBR_SLAB == NBR_GROUP + NA_KH
    scale = HEAD_DIM ** -0.5
    return pl.pallas_call(
        functools.partial(_nbr_kernel, n_rows=n_rows, scale=scale),
        out_shape=jax.ShapeDtypeStruct((batch * seq, GROUP_W), BF16),
        grid=(batch, N_GROUP_HEADS),
        in_specs=[pl.BlockSpec((seq, LANE), lambda b, h: (b, QD_COL + h)),
                  pl.BlockSpec((seq, LANE), lambda b, h: (b, KD_COL + h)),
                  pl.BlockSpec((seq, LANE), lambda b, h: (b, VD_COL + h)),
                  pl.BlockSpec((ctx, LANE), lambda b, h: (b, KD_COL - pc_off + h)),
                  pl.BlockSpec((ctx, LANE), lambda b, h: (b, VD_COL - pc_off + h)),
                  pl.BlockSpec((None, 16, LANE), lambda b, h: (h, 0, 0))],
        out_specs=pl.BlockSpec((seq, LANE), lambda b, h: (b, h)),
        scratch_shapes=[pltpu.VMEM((3, NBR_GROUP * GRID_W, NBR_SLAB * GRID_W), F32)],
        compiler_params=_cparams(("parallel", "parallel")),
        name="attn_nbr",
    )(p, p, p, pc, pc, _nbr_table(na_bias))


def _ctx_kernel(sink_ref, q_ref, k_ref, v_ref, lam_ref, gsub_ref, o_ref, *,
                diff, n_stack, scale, use_sink, lam_init):
    q = q_ref[...]
    q = _stack_diff(q) if diff else _stack_heads(q, n_stack)
    s = _qk(q, k_ref[...])
    if scale != 1.0:
        s = s * scale
    sink = None
    if use_sink:
        sink = _sink_rows(sink_ref, pl.program_id(1), q_ref.shape[0], q.shape[0])
    o = _softmax_attend([(s, v_ref[...])], sink)
    if diff:
        o = _diff_finalize(o, lam_ref, gsub_ref, lam_init)
    else:
        o = _unstack_heads(o, n_stack)
    o_ref[...] = o.astype(o_ref.dtype)


def _ctx_call(q, q_col, k, k_col, v, v_col, *, batch, ctx, n_heads, n_stack, diff=False,
              scale=1.0, sink=None, lam_rows=None, g_sub=None, lam_init=0.0):
    qw = LANE if diff else n_stack * LANE
    qb = q_col * LANE // qw
    if lam_rows is None:
        lam_rows = jnp.zeros((8, LANE), F32)
        g_sub = jnp.ones((HEAD_DIM,), F32)
    use_sink = sink is not None
    if sink is None:
        sink = jnp.zeros((N_GROUP_HEADS,), F32)
    kern = functools.partial(_ctx_kernel, diff=diff, n_stack=n_stack, scale=float(scale),
                             use_sink=use_sink, lam_init=float(lam_init))
    return pl.pallas_call(
        kern,
        out_shape=jax.ShapeDtypeStruct((batch * ctx, GROUP_W), BF16),
        grid=(batch, n_heads),
        in_specs=[pl.BlockSpec(memory_space=pltpu.SMEM),
                  pl.BlockSpec((ctx, qw), lambda b, h: (b, qb + h)),
                  pl.BlockSpec((ctx, LANE), lambda b, h: (b, k_col + h)),
                  pl.BlockSpec((ctx, LANE), lambda b, h: (b, v_col + h)),
                  pl.BlockSpec((8, LANE), lambda b, h: (0, 0)),
                  pl.BlockSpec((1, HEAD_DIM), lambda b, h: (0, 0))],
        out_specs=pl.BlockSpec((ctx, qw), lambda b, h: (b, h)),
        compiler_params=_cparams(("parallel", "parallel")),
        name="attn_ctx",
    )(sink, q, k, v, lam_rows, g_sub.reshape(1, HEAD_DIM))


def _ffn_up_kernel(a_ref, ap_ref, an_ref, wg_ref, wv_ref, cwg_ref, cwv_ref, o_ref, ext_ref, *,
                   tiles_per_seq):
    i = pl.program_id(0)
    tm = a_ref.shape[0]

    @pl.when(pl.program_id(1) == 0)
    def _():
        first = (i % tiles_per_seq) == 0
        last = (i % tiles_per_seq) == tiles_per_seq - 1
        zero = jnp.zeros(ap_ref.shape, ap_ref.dtype)
        ext_ref[0:HALO, :] = jnp.where(first, zero, ap_ref[...])
        ext_ref[HALO:HALO + tm, :] = a_ref[...]
        ext_ref[HALO + tm:, :] = jnp.where(last, zero, an_ref[...])

    a = ext_ref[...]
    ug = jnp.dot(a, wg_ref[...].astype(BF16), preferred_element_type=F32)
    uv = jnp.dot(a, wv_ref[...].astype(BF16), preferred_element_type=F32)
    rows = min(EPI_ROWS, tm)

    def conv(u, r0, cw_ref):
        w = u[HALO + r0 - 8:HALO + r0 + rows + 8]
        n = w.shape[0]
        cu = (pltpu.roll(w, 1, 0) * cw_ref[0:1, :] + w * cw_ref[1:2, :]
              + pltpu.roll(w, n - 1, 0) * cw_ref[2:3, :])
        return cu[8:8 + rows]

    for r0 in range(0, tm, rows):
        gate = conv(ug, r0, cwg_ref)
        val = conv(uv, r0, cwv_ref)
        o_ref[r0:r0 + rows, :] = (gate * jax.nn.sigmoid(gate) * val).astype(o_ref.dtype)


def _ffn_up_call(a, w_up, layer, conv_w, seq):
    m, kdim = a.shape
    f = w_up.shape[2] // 2
    tm = min(1024, seq)
    tn = _pick(f, (256, 128))
    tiles_per_seq = seq // tm
    fb = f // tn
    hb = tm // HALO
    n_halo = m // HALO
    cw = jnp.pad(conv_w, ((0, 8 - CONV_W), (0, 0)))
    return pl.pallas_call(
        functools.partial(_ffn_up_kernel, tiles_per_seq=tiles_per_seq),
        out_shape=jax.ShapeDtypeStruct((m, f), BF16),
        grid=(m // tm, fb),
        in_specs=[pl.BlockSpec((tm, kdim), lambda i, j: (i, 0)),
                  pl.BlockSpec((HALO, kdim), lambda i, j: (jnp.maximum(i * hb - 1, 0), 0)),
                  pl.BlockSpec((HALO, kdim),
                               lambda i, j: (jnp.minimum((i + 1) * hb, n_halo - 1), 0)),
                  pl.BlockSpec((None, kdim, tn), lambda i, j: (layer, 0, j)),
                  pl.BlockSpec((None, kdim, tn), lambda i, j: (layer, 0, fb + j)),
                  pl.BlockSpec((8, tn), lambda i, j: (0, j)),
                  pl.BlockSpec((8, tn), lambda i, j: (0, fb + j))],
        out_specs=pl.BlockSpec((tm, tn), lambda i, j: (i, j)),
        scratch_shapes=[pltpu.VMEM((tm + 2 * HALO, kdim), BF16)],
        compiler_params=_cparams(("parallel", "arbitrary")),
        name="ffn_up",
    )(a, a, a, w_up, w_up, cw, cw)


def _conv_ffn(hx2d, w_up, layer, conv_w, w_down, seq):
    g = _ffn_up_call(hx2d, w_up, layer, conv_w, seq)
    return _matmul_call(g, w_down, F32)


def kernel(x, c, ctx, c_ctx, w_ada, b_ada, g_attn_pre, g_attn_post, g_mlp_pre, g_mlp_post,
           w_in, w_out, g_q_a, g_k_a, lam_q1, lam_k1, lam_q2, lam_k2, g_sub_b, sink_c, na_bias,
           w_up, conv_ffn_w, w_down):
    batch, seq, d = x.shape
    n_ctx = ctx.shape[1]
    depth = w_ada.shape[0]
    assert batch + 1 <= 8 and d == N_Q

    cond = jnp.zeros((8, d), F32).at[:batch].set(c).at[batch].set(c_ctx)
    mod_all = _ada_call(cond, w_ada, b_ada)
    rope_hd = _rope_tables(seq, HEAD_DIM)
    rope_dh = _rope_tables(seq, DH_B)
    scale_hd = HEAD_DIM ** -0.5
    scale_dh = DH_B ** -0.5

    h = ctx
    for l in range(depth):
        last = l == depth - 1
        lam_init = 0.8 - 0.6 * math.exp(-0.3 * l)
        mod = mod_all[l, :batch].reshape(batch, 6, d)
        sh_a, sc_a, gt_a, sh_m, sc_m, gt_m = (mod[:, k] for k in range(6))
        mod_c = mod_all[l, batch].reshape(6, d)
        one = lambda v: v.reshape(1, d)
        wd = w_down[l].astype(BF16)
        lam_rows = jnp.zeros((8, LANE), F32).at[:4, :DH_B].set(
            jnp.stack([lam_q1[l], lam_k1[l], lam_q2[l], lam_k2[l]]))

        hx = _norm_mod_call(x, g_attn_pre[l], sc_a, sh_a).reshape(batch * seq, d)
        hc = _norm_mod_call(h.reshape(1, batch * n_ctx, d), g_attn_pre[l],
                            one(mod_c[1]), one(mod_c[0])).reshape(batch * n_ctx, d)
        p = _matmul_ws_call(hx, w_in, l, BF16)
        pc_off = 32 if last else 0
        pc = _matmul_ws_call(hc, w_in, l, BF16, col0=pc_off * LANE)

        q_a = _prep_call(p, QA_COL, GROUP_W, seq, gain=g_q_a[l], rope=rope_hd,
                         scale=scale_hd * LOG2E)
        k_a = _prep_call(p, KA_COL, KV_GQA * LANE, seq, gain=g_k_a[l], rope=rope_hd)
        kc_a = _prep_call(pc, KA_COL - pc_off, KV_GQA * LANE, n_ctx, gain=g_k_a[l])
        q_b = _prep_call(p, QB_COL, GROUP_W, seq, rope=rope_dh, scale=scale_dh * LOG2E)
        k_b = _prep_call(p, KB_COL, GROUP_W, seq, rope=rope_dh)
        q_c = _prep_call(p, QC_COL, GROUP_W, seq, rope=rope_hd, scale=scale_hd)
        k_c = _prep_call(p, KC_COL, KV_GQA * LANE, seq, rope=rope_hd)

        dims = dict(batch=batch, seq=seq, ctx=n_ctx)
        o_a = _flash_call(q_a, k_a, 0, p, VA_COL, kc_a, 0, pc, VA_COL - pc_off, diff=False, **dims)
        o_b = _flash_call(q_b, k_b, 0, p, VB_COL, pc, KB_COL - pc_off, pc, VB_COL - pc_off,
                          diff=True, lam_rows=lam_rows, g_sub=g_sub_b[l], lam_init=lam_init, **dims)
        o_c = _band_call(sink_c[l], q_c, k_c, p, pc, pc_off, **dims)
        o_d = _nbr_call(p, pc, pc_off, na_bias[l], **dims)
        o = jnp.concatenate([o_a, o_b, o_c, o_d], axis=-1)
        y = _matmul_ws_call(o, w_out, l, F32)
        x = _norm_res_call(x, y, g_attn_post[l], gt_a)

        hx = _norm_mod_call(x, g_mlp_pre[l], sc_m, sh_m).reshape(batch * seq, d)
        y = _conv_ffn(hx, w_up, l, conv_ffn_w[l], wd, seq)
        x = _norm_res_call(x, y, g_mlp_post[l], gt_m)

        if not last:
            cdims = dict(batch=batch, ctx=n_ctx)
            qc_a = _prep_call(pc, QA_COL, GROUP_W, n_ctx, gain=g_q_a[l], scale=scale_hd)
            co_a = _ctx_call(qc_a, 0, kc_a, 0, pc, VA_COL, n_heads=KV_GQA, n_stack=GQA_G, **cdims)
            co_b = _ctx_call(pc, QB_COL, pc, KB_COL, pc, VB_COL, n_heads=N_GROUP_HEADS,
                             n_stack=2, diff=True, scale=scale_dh, lam_rows=lam_rows,
                             g_sub=g_sub_b[l], lam_init=lam_init, **cdims)
            co_c = _ctx_call(pc, QC_COL, pc, KC_COL, pc, VC_COL, n_heads=KV_GQA, n_stack=GQA_G,
                             scale=scale_hd, sink=sink_c[l], **cdims)
            co_d = _ctx_call(pc, QD_COL, pc, KD_COL, pc, VD_COL, n_heads=N_GROUP_HEADS,
                             n_stack=1, scale=scale_hd, **cdims)
            co = jnp.concatenate([co_a, co_b, co_c, co_d], axis=-1)
            yc = _matmul_ws_call(co, w_out, l, F32)
            h3 = h.reshape(1, batch * n_ctx, d)
            h3 = _norm_res_call(h3, yc, g_attn_post[l], one(mod_c[2]))
            hc = _norm_mod_call(h3, g_mlp_pre[l], one(mod_c[4]), one(mod_c[3]))
            yc = _conv_ffn(hc.reshape(batch * n_ctx, d), w_up, l, conv_ffn_w[l], wd, n_ctx)
            h = _norm_res_call(h3, yc, g_mlp_post[l], one(mod_c[5])).reshape(batch, n_ctx, d)
    return x
```

```python
import functools
import math

import jax
import jax.numpy as jnp
import numpy as np
from jax import lax
from jax.experimental import pallas as pl
from jax.experimental.pallas import tpu as pltpu

F32 = jnp.float32
BF16 = jnp.bfloat16

GRID_W = 64
HEAD_DIM = 128
N_GROUP_HEADS = 8
KV_GQA = 2
GQA_G = N_GROUP_HEADS // KV_GQA
DH_B = HEAD_DIM // 2
WIN_C = 128
NA_KH = 8
NA_KW = 16
NBR_GROUP = 8
NBR_SLAB = 16
CONV_W = 3
ROPE_THETA = 10000.0
EPS = 1e-6
NEG_INF = -1e30
LOG2E = math.log2(math.e)
GROUP_W = N_GROUP_HEADS * HEAD_DIM
N_Q = 4 * GROUP_W

QA_COL, QB_COL, QC_COL, QD_COL = 0, 8, 16, 24
KA_COL, VA_COL = 32, 34
KB_COL, VB_COL = 36, 44
KC_COL, VC_COL = 52, 54
KD_COL, VD_COL = 56, 64
KV_COLS = 40

LANE = 128
HALO = 16
EPI_ROWS = 128
VMEM_LIMIT = 56 * 1024 * 1024


def _cparams(sem):
    return pltpu.CompilerParams(dimension_semantics=sem, vmem_limit_bytes=VMEM_LIMIT)


def _ada_kernel(a_ref, w_ref, b_ref, o_ref):
    a = a_ref[...]
    a = a * jax.nn.sigmoid(a)
    o_ref[...] = jnp.dot(a.astype(BF16), w_ref[...].astype(BF16),
                         preferred_element_type=F32) + b_ref[...]


def _ada_call(cond, w_ada, b_ada):
    n_layers, d, n = w_ada.shape
    tn = 512
    return pl.pallas_call(
        _ada_kernel,
        out_shape=jax.ShapeDtypeStruct((n_layers, 8, n), F32),
        grid=(n_layers, n // tn),
        in_specs=[pl.BlockSpec((8, d), lambda l, j: (0, 0)),
                  pl.BlockSpec((None, d, tn), lambda l, j: (l, 0, j)),
                  pl.BlockSpec((None, 1, tn), lambda l, j: (l, 0, j))],
        out_specs=pl.BlockSpec((None, 8, tn), lambda l, j: (l, 0, j)),
        compiler_params=_cparams(("parallel", "parallel")),
        name="ada_mod",
    )(cond, w_ada, b_ada.reshape(n_layers, 1, n))


def _rms(x, g):
    return x * lax.rsqrt(jnp.mean(x * x, axis=-1, keepdims=True) + EPS) * g


def _norm_mod_kernel(x_ref, g_ref, sc_ref, sh_ref, o_ref):
    y = _rms(x_ref[...], g_ref[...])
    o_ref[...] = (y * (1.0 + sc_ref[...]) + sh_ref[...]).astype(o_ref.dtype)


def _norm_mod_call(x, g, sc, sh):
    nb, r, d = x.shape
    tr = min(256, r)
    return pl.pallas_call(
        _norm_mod_kernel,
        out_shape=jax.ShapeDtypeStruct((nb, r, d), BF16),
        grid=(nb, r // tr),
        in_specs=[pl.BlockSpec((None, tr, d), lambda b, i: (b, i, 0)),
                  pl.BlockSpec((1, d), lambda b, i: (0, 0)),
                  pl.BlockSpec((None, 1, d), lambda b, i: (b, 0, 0)),
                  pl.BlockSpec((None, 1, d), lambda b, i: (b, 0, 0))],
        out_specs=pl.BlockSpec((None, tr, d), lambda b, i: (b, i, 0)),
        compiler_params=_cparams(("parallel", "parallel")),
        name="norm_mod",
    )(x, g.reshape(1, d), sc.reshape(nb, 1, d), sh.reshape(nb, 1, d))


def _norm_res_kernel(x_ref, y_ref, g_ref, gt_ref, o_ref):
    o_ref[...] = x_ref[...] + gt_ref[...] * _rms(y_ref[...], g_ref[...])


def _norm_res_call(x, y, g, gate):
    nb, r, d = x.shape
    tr = min(256, r)
    return pl.pallas_call(
        _norm_res_kernel,
        out_shape=jax.ShapeDtypeStruct((nb, r, d), F32),
        grid=(nb, r // tr),
        in_specs=[pl.BlockSpec((None, tr, d), lambda b, i: (b, i, 0)),
                  pl.BlockSpec((None, tr, d), lambda b, i: (b, i, 0)),
                  pl.BlockSpec((1, d), lambda b, i: (0, 0)),
                  pl.BlockSpec((None, 1, d), lambda b, i: (b, 0, 0))],
        out_specs=pl.BlockSpec((None, tr, d), lambda b, i: (b, i, 0)),
        compiler_params=_cparams(("parallel", "parallel")),
        name="norm_res",
    )(x, y.reshape(nb, r, d), g.reshape(1, d), gate.reshape(nb, 1, d))


def _norm_res_mod_kernel(x_ref, y_ref, g_ref, gt_ref, g2_ref, sc_ref, sh_ref, o_ref, h_ref):
    x = x_ref[...] + gt_ref[...] * _rms(y_ref[...], g_ref[...])
    o_ref[...] = x
    h_ref[...] = (_rms(x, g2_ref[...]) * (1.0 + sc_ref[...]) + sh_ref[...]).astype(h_ref.dtype)


def _norm_res_mod_call(x, y, g, gate, g_next, sc, sh):
    nb, r, d = x.shape
    tr = min(256, r)
    row = pl.BlockSpec((None, tr, d), lambda b, i: (b, i, 0))
    vec = pl.BlockSpec((None, 1, d), lambda b, i: (b, 0, 0))
    gain = pl.BlockSpec((1, d), lambda b, i: (0, 0))
    return pl.pallas_call(
        _norm_res_mod_kernel,
        out_shape=(jax.ShapeDtypeStruct((nb, r, d), F32), jax.ShapeDtypeStruct((nb, r, d), BF16)),
        grid=(nb, r // tr),
        in_specs=[row, row, gain, vec, gain, vec, vec],
        out_specs=(row, row),
        compiler_params=_cparams(("parallel", "parallel")),
        name="norm_res_mod",
    )(x, y.reshape(nb, r, d), g.reshape(1, d), gate.reshape(nb, 1, d), g_next.reshape(1, d),
      sc.reshape(nb, 1, d), sh.reshape(nb, 1, d))


def _matmul_kernel(a_ref, w_ref, o_ref, acc_ref, *, nk):
    k = pl.program_id(2)
    part = jnp.dot(a_ref[...], w_ref[...], preferred_element_type=F32)
    if nk == 1:
        o_ref[...] = part.astype(o_ref.dtype)
    else:
        @pl.when(k == 0)
        def _():
            acc_ref[...] = part

        @pl.when(jnp.logical_and(k > 0, k < nk - 1))
        def _():
            acc_ref[...] += part

        @pl.when(k == nk - 1)
        def _():
            o_ref[...] = (acc_ref[...] + part).astype(o_ref.dtype)


def _pick(n, prefs):
    for t in prefs:
        if n % t == 0:
            return t
    return n


def _matmul_call(a, w, out_dtype):
    m, kdim = a.shape
    _, n = w.shape
    tm = _pick(m, (1024, 512, 256))
    tn = _pick(n, (512, 256, 128))
    tk = kdim if kdim <= 4096 else _pick(kdim, (5504, 4096, 2048))
    nk = kdim // tk
    return pl.pallas_call(
        functools.partial(_matmul_kernel, nk=nk),
        out_shape=jax.ShapeDtypeStruct((m, n), out_dtype),
        grid=(m // tm, n // tn, nk),
        in_specs=[pl.BlockSpec((tm, tk), lambda i, j, k: (i, k)),
                  pl.BlockSpec((tk, tn), lambda i, j, k: (k, j))],
        out_specs=pl.BlockSpec((tm, tn), lambda i, j, k: (i, j)),
        scratch_shapes=[pltpu.VMEM((tm, tn) if nk > 1 else (8, LANE), F32)],
        compiler_params=_cparams(("parallel", "parallel", "arbitrary")),
        name="matmul",
    )(a, w)


def _matmul_ws_kernel(a_ref, w_ref, o_ref, wb_ref):
    @pl.when(pl.program_id(1) == 0)
    def _():
        wb_ref[...] = w_ref[...].astype(BF16)

    o_ref[...] = jnp.dot(a_ref[...], wb_ref[...], preferred_element_type=F32).astype(o_ref.dtype)


def _matmul_ws_call(a, w, layer, out_dtype, col0=0):
    m, kdim = a.shape
    n = w.shape[2] - col0
    tm = _pick(m, (1024, 512, 256))
    tn = _pick(n, (512, 256, 128))
    assert col0 % tn == 0
    cb0 = col0 // tn
    return pl.pallas_call(
        _matmul_ws_kernel,
        out_shape=jax.ShapeDtypeStruct((m, n), out_dtype),
        grid=(n // tn, m // tm),
        in_specs=[pl.BlockSpec((tm, kdim), lambda j, i: (i, 0)),
                  pl.BlockSpec((None, kdim, tn), lambda j, i: (layer, 0, cb0 + j))],
        out_specs=pl.BlockSpec((tm, tn), lambda j, i: (i, j)),
        scratch_shapes=[pltpu.VMEM((kdim, tn), BF16)],
        compiler_params=_cparams(("parallel", "arbitrary")),
        name="matmul_ws",
    )(a, w)


def _swap_pairs(y):
    lane = lax.broadcasted_iota(jnp.int32, y.shape, 1)
    nxt = pltpu.roll(y, LANE - 1, 1)
    prv = pltpu.roll(y, 1, 1)
    return jnp.where((lane & 1) == 0, nxt, prv)


def _prep_tile(y, gain, cos, sin, scale):
    y = y.astype(F32)
    if gain is not None:
        y = _rms(y, gain)
    if cos is not None:
        y = y * cos + _swap_pairs(y) * sin
    if scale != 1.0:
        y = y * scale
    return y.astype(BF16)


def _prep_kernel(p_ref, cos_ref, sin_ref, g_ref, o_ref, *, n_chunks, norm, rope, scale):
    for c in range(n_chunks):
        sl = slice(c * LANE, (c + 1) * LANE)
        o_ref[:, sl] = _prep_tile(p_ref[:, sl], g_ref[...] if norm else None,
                                  cos_ref[...] if rope else None,
                                  sin_ref[...] if rope else None, scale)


def _prep_call(p, col0, width, seq, *, gain=None, rope=None, scale=1.0):
    rows = p.shape[0]
    tr = min(256, seq)
    n_seq_tiles = seq // tr
    bw = min(width, 512)
    assert (col0 * LANE) % bw == 0 and width % bw == 0
    cb0 = col0 * LANE // bw
    if rope is None:
        cos = sin = jnp.zeros((tr, LANE), F32)
        tab_map = lambda i, j: (0, 0)
    else:
        cos, sin = rope
        tab_map = lambda i, j: (i % n_seq_tiles, 0)
    g = (jnp.ones((HEAD_DIM,), F32) if gain is None else gain).reshape(1, HEAD_DIM)
    kern = functools.partial(_prep_kernel, n_chunks=bw // LANE, norm=gain is not None,
                             rope=rope is not None, scale=float(scale))
    return pl.pallas_call(
        kern,
        out_shape=jax.ShapeDtypeStruct((rows, width), BF16),
        grid=(rows // tr, width // bw),
        in_specs=[pl.BlockSpec((tr, bw), lambda i, j: (i, cb0 + j)),
                  pl.BlockSpec((tr, LANE), tab_map),
                  pl.BlockSpec((tr, LANE), tab_map),
                  pl.BlockSpec((1, HEAD_DIM), lambda i, j: (0, 0))],
        out_specs=pl.BlockSpec((tr, bw), lambda i, j: (i, j)),
        compiler_params=_cparams(("parallel", "parallel")),
        name="qk_prep",
    )(p, cos, sin, g)


def _rope_tables(seq, head_dim):
    t = jnp.arange(seq)
    rows = (t // GRID_W).astype(F32)
    cols = (t % GRID_W).astype(F32)
    quarter = head_dim // 4
    inv = ROPE_THETA ** (-jnp.arange(quarter, dtype=F32) / quarter)
    ang = jnp.concatenate([rows[:, None] * inv, cols[:, None] * inv], axis=-1)
    lane = np.arange(LANE)
    pair = (lane % head_dim) // 2
    sign = np.where(lane % 2 == 0, -1.0, 1.0).astype(np.float32)
    return jnp.cos(ang)[:, pair], jnp.sin(ang)[:, pair] * sign


def _qk(q, k):
    return lax.dot_general(q, k, (((1,), (1,)), ((), ())), preferred_element_type=F32)


def _stack_heads(q, n):
    if n == 1:
        return q
    return jnp.concatenate([q[:, g * LANE:(g + 1) * LANE] for g in range(n)], axis=0)


def _unstack_heads(o, n):
    if n == 1:
        return o
    t = o.shape[0] // n
    return jnp.concatenate([o[g * t:(g + 1) * t] for g in range(n)], axis=1)


def _stack_diff(q):
    lane = lax.broadcasted_iota(jnp.int32, q.shape, 1)
    zero = jnp.zeros_like(q)
    return jnp.concatenate([jnp.where(lane < DH_B, q, zero),
                            jnp.where(lane >= DH_B, q, zero)], axis=0)


def _softmax_attend(groups, sink=None):
    m = None
    for s, _ in groups:
        sm = jnp.max(s, axis=-1, keepdims=True)
        m = sm if m is None else jnp.maximum(m, sm)
    if sink is not None:
        m = jnp.maximum(m, sink)
    l = None
    acc = None
    for s, v in groups:
        p = jnp.exp(s - m)
        ps = jnp.sum(p, axis=-1, keepdims=True)
        pv = jnp.dot(p.astype(v.dtype), v, preferred_element_type=F32)
        l = ps if l is None else l + ps
        acc = pv if acc is None else acc + pv
    if sink is not None:
        l = l + jnp.exp(sink - m)
    return acc / l


def _diff_lambda(lam_ref, lam_init):
    a = jnp.sum(lam_ref[0:1, :] * lam_ref[1:2, :], axis=-1, keepdims=True)
    b = jnp.sum(lam_ref[2:3, :] * lam_ref[3:4, :], axis=-1, keepdims=True)
    return jnp.exp(a) - jnp.exp(b) + lam_init


def _diff_finalize(o, lam_ref, gsub_ref, lam_init):
    t = o.shape[0] // 2
    d = o[:t] - _diff_lambda(lam_ref, lam_init) * o[t:]
    return _rms(d, gsub_ref[...]) * (1.0 - lam_init)


def _sink_rows(sink_ref, kv, rows_per_head, n_rows):
    head = lax.broadcasted_iota(jnp.int32, (n_rows, 1), 0) // rows_per_head
    out = jnp.zeros((n_rows, 1), F32)
    for g in range(GQA_G):
        out = jnp.where(head == g, sink_ref[kv * GQA_G + g], out)
    return out


def _lane_tiles(s):
    return [s[:, j * LANE:(j + 1) * LANE] for j in range(s.shape[1] // LANE)]


def _flash_kernel(q_ref, kl_ref, vl_ref, kc_ref, vc_ref, cq_ref, sq_ref, ck_ref, sk_ref,
                  gq_ref, gk_ref, lam_ref, gsub_ref, o_ref, kp_ref, kcp_ref, *,
                  diff, n_stack, tk, lam_init, norm, q_scale, q_ready):
    gq = gq_ref[...] if norm else None
    gk = gk_ref[...] if norm else None

    @pl.when(pl.program_id(2) == 0)
    def _():
        for c in range(kl_ref.shape[0] // tk):
            sl = slice(c * tk, (c + 1) * tk)
            kp_ref[sl, :] = _prep_tile(kl_ref[sl, :], gk, ck_ref[sl, :], sk_ref[sl, :], 1.0)
        kcp_ref[...] = _prep_tile(kc_ref[...], gk, None, None, 1.0)

    kl_ref, kc_ref = kp_ref, kcp_ref
    if q_ready:
        q = q_ref[...]
    else:
        cq, sq = cq_ref[...], sq_ref[...]
        q = jnp.concatenate([_prep_tile(q_ref[:, g * LANE:(g + 1) * LANE], gq, cq, sq, q_scale)
                             for g in range(q_ref.shape[1] // LANE)], axis=1)
    q = _stack_diff(q) if diff else _stack_heads(q, n_stack)

    def row_max(tiles):
        m = tiles[0]
        for t in tiles[1:]:
            m = jnp.maximum(m, t)
        return jnp.broadcast_to(jnp.max(m, axis=-1, keepdims=True), m.shape)

    def weights(tiles, m, v):
        ps = [jnp.exp2(t - m) for t in tiles]
        l = ps[0]
        for p in ps[1:]:
            l = l + p
        pv = jnp.dot(jnp.concatenate(ps, axis=1).astype(v.dtype), v, preferred_element_type=F32)
        return l, pv

    def latent_scores(c):
        return _lane_tiles(_qk(q, kl_ref[c * tk:(c + 1) * tk, :]))

    n_chunks = kl_ref.shape[0] // tk
    tiles = _lane_tiles(_qk(q, kc_ref[...]))
    nxt = latent_scores(0)
    m = row_max(tiles)
    l, acc = weights(tiles, m, vc_ref[...])
    for c in range(n_chunks):
        tiles = nxt
        if c + 1 < n_chunks:
            nxt = latent_scores(c + 1)
        m_new = jnp.maximum(m, row_max(tiles))
        alpha = jnp.exp2(m - m_new)
        l_c, pv = weights(tiles, m_new, vl_ref[c * tk:(c + 1) * tk, :])
        l = alpha * l + l_c
        acc = alpha * acc + pv
        m = m_new
    o = acc / jnp.sum(l, axis=-1, keepdims=True)
    if diff:
        o = _diff_finalize(o, lam_ref, gsub_ref, lam_init)
    else:
        o = _unstack_heads(o, n_stack)
    o_ref[...] = o.astype(o_ref.dtype)


def _flash_call(p, pc, pc_off, q_col, k_col, v_col, rope, q_scale, *, batch, seq, ctx, diff,
                q_ready=None, g_q=None, g_k=None, lam_rows=None, g_sub=None, lam_init=0.0):
    n_heads = N_GROUP_HEADS if diff else KV_GQA
    n_stack = 2 if diff else GQA_G
    tq = min(512 // n_stack, seq)
    qw = LANE if diff else GQA_G * LANE
    qb = 0 if q_ready is not None else q_col * LANE // qw
    nq = seq // tq
    norm = g_k is not None
    ones = jnp.ones((HEAD_DIM,), F32)
    if lam_rows is None:
        lam_rows = jnp.zeros((8, LANE), F32)
    vec = lambda g: (ones if g is None else g).reshape(1, HEAD_DIM)
    cos, sin = rope
    kern = functools.partial(_flash_kernel, diff=diff, n_stack=n_stack, tk=min(512, seq),
                             lam_init=float(lam_init), norm=norm, q_scale=float(q_scale),
                             q_ready=q_ready is not None)
    const = lambda b, h, i: (0, 0)
    return pl.pallas_call(
        kern,
        out_shape=jax.ShapeDtypeStruct((batch * seq, GROUP_W), BF16),
        grid=(batch, n_heads, nq),
        in_specs=[pl.BlockSpec((tq, qw), lambda b, h, i: (b * nq + i, qb + h)),
                  pl.BlockSpec((seq, LANE), lambda b, h, i: (b, k_col + h)),
                  pl.BlockSpec((seq, LANE), lambda b, h, i: (b, v_col + h)),
                  pl.BlockSpec((ctx, LANE), lambda b, h, i: (b, k_col - pc_off + h)),
                  pl.BlockSpec((ctx, LANE), lambda b, h, i: (b, v_col - pc_off + h)),
                  pl.BlockSpec((tq, LANE), lambda b, h, i: (i, 0)),
                  pl.BlockSpec((tq, LANE), lambda b, h, i: (i, 0)),
                  pl.BlockSpec((seq, LANE), const),
                  pl.BlockSpec((seq, LANE), const),
                  pl.BlockSpec((1, HEAD_DIM), const),
                  pl.BlockSpec((1, HEAD_DIM), const),
                  pl.BlockSpec((8, LANE), const),
                  pl.BlockSpec((1, HEAD_DIM), const)],
        out_specs=pl.BlockSpec((tq, qw), lambda b, h, i: (b * nq + i, h)),
        scratch_shapes=[pltpu.VMEM((seq, LANE), BF16), pltpu.VMEM((ctx, LANE), BF16)],
        compiler_params=_cparams(("parallel", "parallel", "arbitrary")),
        name="attn_full",
    )(p if q_ready is None else q_ready, p, p, pc, pc, cos, sin, cos, sin, vec(g_q), vec(g_k),
      lam_rows, vec(g_sub))


def _band_kernel(sink_ref, q_ref, kp_ref, kx_ref, kn_ref, vp_ref, vx_ref, vn_ref,
                 kc_ref, vc_ref, o_ref, *, tq, seq):
    kv = pl.program_id(1)
    i = pl.program_id(2)
    q = _stack_heads(q_ref[...], GQA_G)
    rows = q.shape[0]
    k_loc = jnp.concatenate([kp_ref[...], kx_ref[...], kn_ref[...]], axis=0)
    v_loc = jnp.concatenate([vp_ref[...], vx_ref[...], vn_ref[...]], axis=0)
    s_loc = _qk(q, k_loc)
    qpos = i * tq + (lax.broadcasted_iota(jnp.int32, s_loc.shape, 0) & (tq - 1))
    kpos = (i - 1) * tq + lax.broadcasted_iota(jnp.int32, s_loc.shape, 1)
    ok = jnp.where(kpos >= 0, jnp.abs(qpos - kpos), WIN_C + 1)
    ok = jnp.where(kpos < seq, ok, WIN_C + 1) <= WIN_C
    s_loc = jnp.where(ok, s_loc, NEG_INF)
    s_ctx = _qk(q, kc_ref[...])
    sink = _sink_rows(sink_ref, kv, tq, rows)
    o = _softmax_attend([(s_loc, v_loc), (s_ctx, vc_ref[...])], sink)
    o_ref[...] = _unstack_heads(o, GQA_G).astype(o_ref.dtype)


def _band_call(sink, q, kl, p, pc, pc_off, *, batch, seq, ctx):
    tq = WIN_C
    nq = seq // tq
    qw = GQA_G * LANE

    def nb(delta, col):
        def imap(b, h, i):
            return (b * nq + jnp.clip(i + delta, 0, nq - 1), col + h)
        return imap

    return pl.pallas_call(
        functools.partial(_band_kernel, tq=tq, seq=seq),
        out_shape=jax.ShapeDtypeStruct((batch * seq, GROUP_W), BF16),
        grid=(batch, KV_GQA, nq),
        in_specs=[pl.BlockSpec(memory_space=pltpu.SMEM),
                  pl.BlockSpec((tq, qw), lambda b, h, i: (b * nq + i, h)),
                  pl.BlockSpec((tq, LANE), nb(-1, 0)),
                  pl.BlockSpec((tq, LANE), nb(0, 0)),
                  pl.BlockSpec((tq, LANE), nb(1, 0)),
                  pl.BlockSpec((tq, LANE), nb(-1, VC_COL)),
                  pl.BlockSpec((tq, LANE), nb(0, VC_COL)),
                  pl.BlockSpec((tq, LANE), nb(1, VC_COL)),
                  pl.BlockSpec((ctx, LANE), lambda b, h, i: (b, KC_COL - pc_off + h)),
                  pl.BlockSpec((ctx, LANE), lambda b, h, i: (b, VC_COL - pc_off + h))],
        out_specs=pl.BlockSpec((tq, qw), lambda b, h, i: (b * nq + i, h)),
        compiler_params=_cparams(("parallel", "parallel", "parallel")),
        name="attn_band",
    )(sink, q, kl, kl, kl, p, p, p, pc, pc)


def _nbr_build_bias(tab_ref, bias_ref):
    shape = (GRID_W, LANE)
    w = lax.broadcasted_iota(jnp.int32, shape, 0)
    lane = lax.broadcasted_iota(jnp.int32, shape, 1)
    c = lane & (GRID_W - 1)
    cs = jnp.clip(w - NA_KW // 2, 0, GRID_W - NA_KW)
    d = c - cs
    mask = jnp.where(d >= 0, jnp.where(d < NA_KW, 0.0, NEG_INF), NEG_INF).astype(F32)
    neg = jnp.full(shape, NEG_INF, F32)
    toep = [pltpu.roll(jnp.broadcast_to(tab_ref[ro:ro + 1, :], shape), LANE - (NA_KW - 1), 1,
                       stride=1, stride_axis=0) + mask for ro in range(2 * NA_KH - 1)]
    kinds = (lambda dr: (dr, max(dr - NA_KH // 2, 0)),
             lambda dr: (NA_KH // 2 + dr, dr),
             lambda dr: (NBR_GROUP + dr, min(NA_KH // 2 + dr, NBR_SLAB - NA_KH)))
    for kind, rel in enumerate(kinds):
        for dr in range(NBR_GROUP):
            r_rel, rs_rel = rel(dr)
            for j in range(NBR_SLAB // 2):
                halves = [toep[kr - r_rel + NA_KH - 1] if rs_rel <= kr < rs_rel + NA_KH else neg
                          for kr in (2 * j, 2 * j + 1)]
                tile = neg if halves[0] is neg and halves[1] is neg else jnp.where(
                    lane < GRID_W, halves[0], halves[1])
                bias_ref[kind, dr * GRID_W:(dr + 1) * GRID_W, j * LANE:(j + 1) * LANE] = tile


def _nbr_kernel(q_ref, k_ref, v_ref, kc_ref, vc_ref, tab_ref, o_ref, bias_ref, *, n_rows, scale):
    n_groups = n_rows // NBR_GROUP
    gq = NBR_GROUP * GRID_W
    slab = NBR_SLAB * GRID_W
    kc = kc_ref[...]
    vc = vc_ref[...]
    _nbr_build_bias(tab_ref, bias_ref)

    def keys_of(g):
        base = min(max(g * NBR_GROUP - NA_KH // 2, 0), n_rows - NBR_SLAB)
        return slice(base * GRID_W, base * GRID_W + slab)

    def scores(g):
        q = q_ref[g * gq:(g + 1) * gq, :]
        return _qk(q, k_ref[keys_of(g), :]), _qk(q, kc)

    nxt = scores(0)
    for g in range(n_groups):
        raw_loc, raw_ctx = nxt
        if g + 1 < n_groups:
            nxt = scores(g + 1)
        kind = 0 if g == 0 else (2 if g == n_groups - 1 else 1)
        s_loc = raw_loc * scale + bias_ref[kind]
        o = _softmax_attend([(s_loc, v_ref[keys_of(g), :]), (raw_ctx * scale, vc)])
        o_ref[g * gq:(g + 1) * gq, :] = o.astype(o_ref.dtype)


def _nbr_table(na_bias):
    h, nr, nc = na_bias.shape
    t = jnp.pad(na_bias.astype(F32), ((0, 0), (0, 16 - nr), (0, GRID_W - nc)))
    return jnp.concatenate([t] * (LANE // GRID_W), axis=-1)


def _nbr_call(p, pc, pc_off, na_bias, *, batch, seq, ctx):
    n_rows = seq // GRID_W
    assert n_rows >= NBR_SLAB and n_rows % NBR_GROUP == 0 and 2 * GRID_W == LANE
    assert NBR_SLAB >= NBR_GROUP + NA_KH - 1 and NBR_SLAB == NBR_GROUP + NA_KH
    scale = HEAD_DIM ** -0.5
    return pl.pallas_call(
        functools.partial(_nbr_kernel, n_rows=n_rows, scale=scale),
        out_shape=jax.ShapeDtypeStruct((batch * seq, GROUP_W), BF16),
        grid=(batch, N_GROUP_HEADS),
        in_specs=[pl.BlockSpec((seq, LANE), lambda b, h: (b, QD_COL + h)),
                  pl.BlockSpec((seq, LANE), lambda b, h: (b, KD_COL + h)),
                  pl.BlockSpec((seq, LANE), lambda b, h: (b, VD_COL + h)),
                  pl.BlockSpec((ctx, LANE), lambda b, h: (b, KD_COL - pc_off + h)),
                  pl.BlockSpec((ctx, LANE), lambda b, h: (b, VD_COL - pc_off + h)),
                  pl.BlockSpec((None, 16, LANE), lambda b, h: (h, 0, 0))],
        out_specs=pl.BlockSpec((seq, LANE), lambda b, h: (b, h)),
        scratch_shapes=[pltpu.VMEM((3, NBR_GROUP * GRID_W, NBR_SLAB * GRID_W), F32)],
        compiler_params=_cparams(("parallel", "parallel")),
        name="attn_nbr",
    )(p, p, p, pc, pc, _nbr_table(na_bias))


def _ctx_kernel(sink_ref, q_ref, k_ref, v_ref, lam_ref, gsub_ref, o_ref, *,
                diff, n_stack, scale, use_sink, lam_init):
    q = q_ref[...]
    q = _stack_diff(q) if diff else _stack_heads(q, n_stack)
    s = _qk(q, k_ref[...])
    if scale != 1.0:
        s = s * scale
    sink = None
    if use_sink:
        sink = _sink_rows(sink_ref, pl.program_id(1), q_ref.shape[0], q.shape[0])
    o = _softmax_attend([(s, v_ref[...])], sink)
    if diff:
        o = _diff_finalize(o, lam_ref, gsub_ref, lam_init)
    else:
        o = _unstack_heads(o, n_stack)
    o_ref[...] = o.astype(o_ref.dtype)


def _ctx_call(q, q_col, k, k_col, v, v_col, *, batch, ctx, n_heads, n_stack, diff=False,
              scale=1.0, sink=None, lam_rows=None, g_sub=None, lam_init=0.0):
    qw = LANE if diff else n_stack * LANE
    qb = q_col * LANE // qw
    if lam_rows is None:
        lam_rows = jnp.zeros((8, LANE), F32)
        g_sub = jnp.ones((HEAD_DIM,), F32)
    use_sink = sink is not None
    if sink is None:
        sink = jnp.zeros((N_GROUP_HEADS,), F32)
    kern = functools.partial(_ctx_kernel, diff=diff, n_stack=n_stack, scale=float(scale),
                             use_sink=use_sink, lam_init=float(lam_init))
    return pl.pallas_call(
        kern,
        out_shape=jax.ShapeDtypeStruct((batch * ctx, GROUP_W), BF16),
        grid=(batch, n_heads),
        in_specs=[pl.BlockSpec(memory_space=pltpu.SMEM),
                  pl.BlockSpec((ctx, qw), lambda b, h: (b, qb + h)),
                  pl.BlockSpec((ctx, LANE), lambda b, h: (b, k_col + h)),
                  pl.BlockSpec((ctx, LANE), lambda b, h: (b, v_col + h)),
                  pl.BlockSpec((8, LANE), lambda b, h: (0, 0)),
                  pl.BlockSpec((1, HEAD_DIM), lambda b, h: (0, 0))],
        out_specs=pl.BlockSpec((ctx, qw), lambda b, h: (b, h)),
        compiler_params=_cparams(("parallel", "parallel")),
        name="attn_ctx",
    )(sink, q, k, v, lam_rows, g_sub.reshape(1, HEAD_DIM))


def _ffn_up_kernel(a_ref, ap_ref, an_ref, wg_ref, wv_ref, cwg_ref, cwv_ref, o_ref, ext_ref, *,
                   tiles_per_seq):
    i = pl.program_id(0)
    tm = a_ref.shape[0]

    @pl.when(pl.program_id(1) == 0)
    def _():
        first = (i % tiles_per_seq) == 0
        last = (i % tiles_per_seq) == tiles_per_seq - 1
        zero = jnp.zeros(ap_ref.shape, ap_ref.dtype)
        ext_ref[0:HALO, :] = jnp.where(first, zero, ap_ref[...])
        ext_ref[HALO:HALO + tm, :] = a_ref[...]
        ext_ref[HALO + tm:, :] = jnp.where(last, zero, an_ref[...])

    a = ext_ref[...]
    rows = min(EPI_ROWS, tm)

    def conv(u, r0, cw_ref):
        w = u[HALO + r0 - 8:HALO + r0 + rows + 8]
        n = w.shape[0]
        cu = (pltpu.roll(w, 1, 0) * cw_ref[0:1, :] + w * cw_ref[1:2, :]
              + pltpu.roll(w, n - 1, 0) * cw_ref[2:3, :])
        return cu[8:8 + rows]

    ug = jnp.dot(a, wg_ref[...].astype(BF16), preferred_element_type=F32)
    uv = jnp.dot(a, wv_ref[...].astype(BF16), preferred_element_type=F32)
    for r0 in range(0, tm, rows):
        gate = conv(ug, r0, cwg_ref)
        val = conv(uv, r0, cwv_ref)
        o_ref[r0:r0 + rows, :] = (gate * jax.nn.sigmoid(gate) * val).astype(o_ref.dtype)


def _ffn_up_call(a, w_up, layer, conv_w, seq):
    m, kdim = a.shape
    f = w_up.shape[2] // 2
    tm = min(1024, seq)
    tn = _pick(f, (256, 128))
    tiles_per_seq = seq // tm
    fb = f // tn
    hb = tm // HALO
    n_halo = m // HALO
    cw = jnp.pad(conv_w, ((0, 8 - CONV_W), (0, 0)))
    return pl.pallas_call(
        functools.partial(_ffn_up_kernel, tiles_per_seq=tiles_per_seq),
        out_shape=jax.ShapeDtypeStruct((m, f), BF16),
        grid=(m // tm, fb),
        in_specs=[pl.BlockSpec((tm, kdim), lambda i, j: (i, 0)),
                  pl.BlockSpec((HALO, kdim), lambda i, j: (jnp.maximum(i * hb - 1, 0), 0)),
                  pl.BlockSpec((HALO, kdim),
                               lambda i, j: (jnp.minimum((i + 1) * hb, n_halo - 1), 0)),
                  pl.BlockSpec((None, kdim, tn), lambda i, j: (layer, 0, j)),
                  pl.BlockSpec((None, kdim, tn), lambda i, j: (layer, 0, fb + j)),
                  pl.BlockSpec((8, tn), lambda i, j: (0, j)),
                  pl.BlockSpec((8, tn), lambda i, j: (0, fb + j))],
        out_specs=pl.BlockSpec((tm, tn), lambda i, j: (i, j)),
        scratch_shapes=[pltpu.VMEM((tm + 2 * HALO, kdim), BF16)],
        compiler_params=_cparams(("parallel", "arbitrary")),
        name="ffn_up",
    )(a, a, a, w_up, w_up, cw, cw)


def _conv_ffn(hx2d, w_up, layer, conv_w, w_down, seq):
    g = _ffn_up_call(hx2d, w_up, layer, conv_w, seq)
    return _matmul_call(g, w_down, F32)


def kernel(x, c, ctx, c_ctx, w_ada, b_ada, g_attn_pre, g_attn_post, g_mlp_pre, g_mlp_post,
           w_in, w_out, g_q_a, g_k_a, lam_q1, lam_k1, lam_q2, lam_k2, g_sub_b, sink_c, na_bias,
           w_up, conv_ffn_w, w_down):
    batch, seq, d = x.shape
    n_ctx = ctx.shape[1]
    depth = w_ada.shape[0]
    assert batch + 1 <= 8 and d == N_Q

    cond = jnp.zeros((8, d), F32).at[:batch].set(c).at[batch].set(c_ctx)
    mod_all = _ada_call(cond, w_ada, b_ada)
    rope_hd = _rope_tables(seq, HEAD_DIM)
    rope_dh = _rope_tables(seq, DH_B)
    scale_hd = HEAD_DIM ** -0.5
    scale_dh = DH_B ** -0.5

    def latent_mod(l):
        mod = mod_all[l, :batch].reshape(batch, 6, d)
        return tuple(mod[:, k] for k in range(6))

    h = ctx
    sh_a, sc_a, gt_a, sh_m, sc_m, gt_m = latent_mod(0)
    hx = _norm_mod_call(x, g_attn_pre[0], sc_a, sh_a)
    for l in range(depth):
        last = l == depth - 1
        lam_init = 0.8 - 0.6 * math.exp(-0.3 * l)
        sh_a, sc_a, gt_a, sh_m, sc_m, gt_m = latent_mod(l)
        mod_c = mod_all[l, batch].reshape(6, d)
        one = lambda v: v.reshape(1, d)
        wd = w_down[l].astype(BF16)
        lam_rows = jnp.zeros((8, LANE), F32).at[:4, :DH_B].set(
            jnp.stack([lam_q1[l], lam_k1[l], lam_q2[l], lam_k2[l]]))

        hx = hx.reshape(batch * seq, d)
        hc = _norm_mod_call(h.reshape(1, batch * n_ctx, d), g_attn_pre[l],
                            one(mod_c[1]), one(mod_c[0])).reshape(batch * n_ctx, d)
        p = _matmul_ws_call(hx, w_in, l, BF16)
        pc_off = 32 if last else 0
        pc = _matmul_ws_call(hc, w_in, l, BF16, col0=pc_off * LANE)

        dims = dict(batch=batch, seq=seq, ctx=n_ctx)
        q_c = _prep_call(p, QC_COL, GROUP_W, seq, rope=rope_hd, scale=scale_hd)
        k_c = _prep_call(p, KC_COL, KV_GQA * LANE, seq, rope=rope_hd)
        q_a = _prep_call(p, QA_COL, GROUP_W, seq, gain=g_q_a[l], rope=rope_hd,
                         scale=scale_hd * LOG2E)
        o_a = _flash_call(p, pc, pc_off, QA_COL, KA_COL, VA_COL, rope_hd, 1.0, diff=False,
                          q_ready=q_a, g_k=g_k_a[l], **dims)
        o_b = _flash_call(p, pc, pc_off, QB_COL, KB_COL, VB_COL, rope_dh, scale_dh * LOG2E,
                          diff=True, lam_rows=lam_rows, g_sub=g_sub_b[l], lam_init=lam_init, **dims)
        o_c = _band_call(sink_c[l], q_c, k_c, p, pc, pc_off, **dims)
        o_d = _nbr_call(p, pc, pc_off, na_bias[l], **dims)
        o = jnp.concatenate([o_a, o_b, o_c, o_d], axis=-1)
        y = _matmul_ws_call(o, w_out, l, F32)
        x, hx = _norm_res_mod_call(x, y, g_attn_post[l], gt_a, g_mlp_pre[l], sc_m, sh_m)

        y = _conv_ffn(hx.reshape(batch * seq, d), w_up, l, conv_ffn_w[l], wd, seq)
        if last:
            x = _norm_res_call(x, y, g_mlp_post[l], gt_m)
        else:
            nxt = latent_mod(l + 1)
            x, hx = _norm_res_mod_call(x, y, g_mlp_post[l], gt_m, g_attn_pre[l + 1], nxt[1], nxt[0])

        if not last:
            cdims = dict(batch=batch, ctx=n_ctx)
            qc_a = _prep_call(pc, QA_COL, GROUP_W, n_ctx, gain=g_q_a[l], scale=scale_hd)
            kc_a = _prep_call(pc, KA_COL, KV_GQA * LANE, n_ctx, gain=g_k_a[l])
            co_a = _ctx_call(qc_a, 0, kc_a, 0, pc, VA_COL, n_heads=KV_GQA, n_stack=GQA_G, **cdims)
            co_b = _ctx_call(pc, QB_COL, pc, KB_COL, pc, VB_COL, n_heads=N_GROUP_HEADS,
                             n_stack=2, diff=True, scale=scale_dh, lam_rows=lam_rows,
                             g_sub=g_sub_b[l], lam_init=lam_init, **cdims)
            co_c = _ctx_call(pc, QC_COL, pc, KC_COL, pc, VC_COL, n_heads=KV_GQA, n_stack=GQA_G,
                             scale=scale_hd, sink=sink_c[l], **cdims)
            co_d = _ctx_call(pc, QD_COL, pc, KD_COL, pc, VD_COL, n_heads=N_GROUP_HEADS,
                             n_stack=1, scale=scale_hd, **cdims)
            co = jnp.concatenate([co_a, co_b, co_c, co_d], axis=-1)
            yc = _matmul_ws_call(co, w_out, l, F32)
            h3 = h.reshape(1, batch * n_ctx, d)
            h3 = _norm_res_call(h3, yc, g_attn_post[l], one(mod_c[2]))
            hc = _norm_mod_call(h3, g_mlp_pre[l], one(mod_c[4]), one(mod_c[3]))
            yc = _conv_ffn(hc.reshape(batch * n_ctx, d), w_up, l, conv_ffn_w[l], wd, n_ctx)
            h = _norm_res_call(h3, yc, g_mlp_post[l], one(mod_c[5])).reshape(batch, n_ctx, d)
    return x
```

```python
import functools
import math

import jax
import jax.numpy as jnp
import numpy as np
from jax import lax
from jax.experimental import pallas as pl
from jax.experimental.pallas import tpu as pltpu

F32 = jnp.float32
BF16 = jnp.bfloat16

GRID_W = 64
HEAD_DIM = 128
N_GROUP_HEADS = 8
KV_GQA = 2
GQA_G = N_GROUP_HEADS // KV_GQA
DH_B = HEAD_DIM // 2
WIN_C = 128
NA_KH = 8
NA_KW = 16
NBR_GROUP = 8
NBR_SLAB = 16
CONV_W = 3
ROPE_THETA = 10000.0
EPS = 1e-6
NEG_INF = -1e30
LOG2E = math.log2(math.e)
GROUP_W = N_GROUP_HEADS * HEAD_DIM
N_Q = 4 * GROUP_W

QA_COL, QB_COL, QC_COL, QD_COL = 0, 8, 16, 24
KA_COL, VA_COL = 32, 34
KB_COL, VB_COL = 36, 44
KC_COL, VC_COL = 52, 54
KD_COL, VD_COL = 56, 64
KV_COLS = 40

LANE = 128
HALO = 16
EPI_ROWS = 128
VMEM_LIMIT = 56 * 1024 * 1024


def _cparams(sem):
    return pltpu.CompilerParams(dimension_semantics=sem, vmem_limit_bytes=VMEM_LIMIT)


def _ada_kernel(a_ref, w_ref, b_ref, o_ref):
    a = a_ref[...]
    a = a * jax.nn.sigmoid(a)
    o_ref[...] = jnp.dot(a.astype(BF16), w_ref[...].astype(BF16),
                         preferred_element_type=F32) + b_ref[...]


def _ada_call(cond, w_ada, b_ada):
    n_layers, d, n = w_ada.shape
    tn = 512
    return pl.pallas_call(
        _ada_kernel,
        out_shape=jax.ShapeDtypeStruct((n_layers, 8, n), F32),
        grid=(n_layers, n // tn),
        in_specs=[pl.BlockSpec((8, d), lambda l, j: (0, 0)),
                  pl.BlockSpec((None, d, tn), lambda l, j: (l, 0, j)),
                  pl.BlockSpec((None, 1, tn), lambda l, j: (l, 0, j))],
        out_specs=pl.BlockSpec((None, 8, tn), lambda l, j: (l, 0, j)),
        compiler_params=_cparams(("parallel", "parallel")),
        name="ada_mod",
    )(cond, w_ada, b_ada.reshape(n_layers, 1, n))


def _rms(x, g):
    return x * lax.rsqrt(jnp.mean(x * x, axis=-1, keepdims=True) + EPS) * g


def _norm_mod_kernel(x_ref, g_ref, sc_ref, sh_ref, o_ref):
    y = _rms(x_ref[...], g_ref[...])
    o_ref[...] = (y * (1.0 + sc_ref[...]) + sh_ref[...]).astype(o_ref.dtype)


def _norm_mod_call(x, g, sc, sh):
    nb, r, d = x.shape
    tr = min(256, r)
    return pl.pallas_call(
        _norm_mod_kernel,
        out_shape=jax.ShapeDtypeStruct((nb, r, d), BF16),
        grid=(nb, r // tr),
        in_specs=[pl.BlockSpec((None, tr, d), lambda b, i: (b, i, 0)),
                  pl.BlockSpec((1, d), lambda b, i: (0, 0)),
                  pl.BlockSpec((None, 1, d), lambda b, i: (b, 0, 0)),
                  pl.BlockSpec((None, 1, d), lambda b, i: (b, 0, 0))],
        out_specs=pl.BlockSpec((None, tr, d), lambda b, i: (b, i, 0)),
        compiler_params=_cparams(("parallel", "parallel")),
        name="norm_mod",
    )(x, g.reshape(1, d), sc.reshape(nb, 1, d), sh.reshape(nb, 1, d))


def _norm_res_kernel(x_ref, y_ref, g_ref, gt_ref, o_ref):
    o_ref[...] = x_ref[...] + gt_ref[...] * _rms(y_ref[...], g_ref[...])


def _norm_res_call(x, y, g, gate):
    nb, r, d = x.shape
    tr = min(256, r)
    return pl.pallas_call(
        _norm_res_kernel,
        out_shape=jax.ShapeDtypeStruct((nb, r, d), F32),
        grid=(nb, r // tr),
        in_specs=[pl.BlockSpec((None, tr, d), lambda b, i: (b, i, 0)),
                  pl.BlockSpec((None, tr, d), lambda b, i: (b, i, 0)),
                  pl.BlockSpec((1, d), lambda b, i: (0, 0)),
                  pl.BlockSpec((None, 1, d), lambda b, i: (b, 0, 0))],
        out_specs=pl.BlockSpec((None, tr, d), lambda b, i: (b, i, 0)),
        compiler_params=_cparams(("parallel", "parallel")),
        name="norm_res",
    )(x, y.reshape(nb, r, d), g.reshape(1, d), gate.reshape(nb, 1, d))


def _norm_res_mod_kernel(x_ref, y_ref, g_ref, gt_ref, g2_ref, sc_ref, sh_ref, o_ref, h_ref):
    x = x_ref[...] + gt_ref[...] * _rms(y_ref[...], g_ref[...])
    o_ref[...] = x
    h_ref[...] = (_rms(x, g2_ref[...]) * (1.0 + sc_ref[...]) + sh_ref[...]).astype(h_ref.dtype)


def _norm_res_mod_call(x, y, g, gate, g_next, sc, sh):
    nb, r, d = x.shape
    tr = min(256, r)
    row = pl.BlockSpec((None, tr, d), lambda b, i: (b, i, 0))
    vec = pl.BlockSpec((None, 1, d), lambda b, i: (b, 0, 0))
    gain = pl.BlockSpec((1, d), lambda b, i: (0, 0))
    return pl.pallas_call(
        _norm_res_mod_kernel,
        out_shape=(jax.ShapeDtypeStruct((nb, r, d), F32), jax.ShapeDtypeStruct((nb, r, d), BF16)),
        grid=(nb, r // tr),
        in_specs=[row, row, gain, vec, gain, vec, vec],
        out_specs=(row, row),
        compiler_params=_cparams(("parallel", "parallel")),
        name="norm_res_mod",
    )(x, y.reshape(nb, r, d), g.reshape(1, d), gate.reshape(nb, 1, d), g_next.reshape(1, d),
      sc.reshape(nb, 1, d), sh.reshape(nb, 1, d))


def _matmul_kernel(a_ref, w_ref, o_ref, acc_ref, *, nk):
    k = pl.program_id(2)
    part = jnp.dot(a_ref[...], w_ref[...], preferred_element_type=F32)
    if nk == 1:
        o_ref[...] = part.astype(o_ref.dtype)
    else:
        @pl.when(k == 0)
        def _():
            acc_ref[...] = part

        @pl.when(jnp.logical_and(k > 0, k < nk - 1))
        def _():
            acc_ref[...] += part

        @pl.when(k == nk - 1)
        def _():
            o_ref[...] = (acc_ref[...] + part).astype(o_ref.dtype)


def _pick(n, prefs):
    for t in prefs:
        if n % t == 0:
            return t
    return n


def _matmul_call(a, w, out_dtype):
    m, kdim = a.shape
    _, n = w.shape
    tm = _pick(m, (1024, 512, 256))
    tn = _pick(n, (512, 256, 128))
    tk = kdim if kdim <= 4096 else _pick(kdim, (5504, 4096, 2048))
    nk = kdim // tk
    return pl.pallas_call(
        functools.partial(_matmul_kernel, nk=nk),
        out_shape=jax.ShapeDtypeStruct((m, n), out_dtype),
        grid=(m // tm, n // tn, nk),
        in_specs=[pl.BlockSpec((tm, tk), lambda i, j, k: (i, k)),
                  pl.BlockSpec((tk, tn), lambda i, j, k: (k, j))],
        out_specs=pl.BlockSpec((tm, tn), lambda i, j, k: (i, j)),
        scratch_shapes=[pltpu.VMEM((tm, tn) if nk > 1 else (8, LANE), F32)],
        compiler_params=_cparams(("parallel", "parallel", "arbitrary")),
        name="matmul",
    )(a, w)


def _matmul_ws_kernel(a_ref, w_ref, o_ref, wb_ref):
    @pl.when(pl.program_id(1) == 0)
    def _():
        wb_ref[...] = w_ref[...].astype(BF16)

    o_ref[...] = jnp.dot(a_ref[...], wb_ref[...], preferred_element_type=F32).astype(o_ref.dtype)


def _matmul_ws_call(a, w, layer, out_dtype, col0=0):
    m, kdim = a.shape
    n = w.shape[2] - col0
    tm = _pick(m, (1024, 512, 256))
    tn = _pick(n, (512, 256, 128))
    assert col0 % tn == 0
    cb0 = col0 // tn
    return pl.pallas_call(
        _matmul_ws_kernel,
        out_shape=jax.ShapeDtypeStruct((m, n), out_dtype),
        grid=(n // tn, m // tm),
        in_specs=[pl.BlockSpec((tm, kdim), lambda j, i: (i, 0)),
                  pl.BlockSpec((None, kdim, tn), lambda j, i: (layer, 0, cb0 + j))],
        out_specs=pl.BlockSpec((tm, tn), lambda j, i: (i, j)),
        scratch_shapes=[pltpu.VMEM((kdim, tn), BF16)],
        compiler_params=_cparams(("parallel", "arbitrary")),
        name="matmul_ws",
    )(a, w)


def _swap_pairs(y):
    lane = lax.broadcasted_iota(jnp.int32, y.shape, 1)
    nxt = pltpu.roll(y, LANE - 1, 1)
    prv = pltpu.roll(y, 1, 1)
    return jnp.where((lane & 1) == 0, nxt, prv)


def _prep_tile(y, gain, cos, sin, scale):
    y = y.astype(F32)
    if gain is not None:
        y = _rms(y, gain)
    if cos is not None:
        y = y * cos + _swap_pairs(y) * sin
    if scale != 1.0:
        y = y * scale
    return y.astype(BF16)


def _prep_kernel(p_ref, cos_ref, sin_ref, g_ref, o_ref, *, n_chunks, norm, rope, scale):
    for c in range(n_chunks):
        sl = slice(c * LANE, (c + 1) * LANE)
        o_ref[:, sl] = _prep_tile(p_ref[:, sl], g_ref[...] if norm else None,
                                  cos_ref[...] if rope else None,
                                  sin_ref[...] if rope else None, scale)


def _prep_call(p, col0, width, seq, *, gain=None, rope=None, scale=1.0):
    rows = p.shape[0]
    tr = min(256, seq)
    n_seq_tiles = seq // tr
    bw = min(width, 512)
    assert (col0 * LANE) % bw == 0 and width % bw == 0
    cb0 = col0 * LANE // bw
    if rope is None:
        cos = sin = jnp.zeros((tr, LANE), F32)
        tab_map = lambda i, j: (0, 0)
    else:
        cos, sin = rope
        tab_map = lambda i, j: (i % n_seq_tiles, 0)
    g = (jnp.ones((HEAD_DIM,), F32) if gain is None else gain).reshape(1, HEAD_DIM)
    kern = functools.partial(_prep_kernel, n_chunks=bw // LANE, norm=gain is not None,
                             rope=rope is not None, scale=float(scale))
    return pl.pallas_call(
        kern,
        out_shape=jax.ShapeDtypeStruct((rows, width), BF16),
        grid=(rows // tr, width // bw),
        in_specs=[pl.BlockSpec((tr, bw), lambda i, j: (i, cb0 + j)),
                  pl.BlockSpec((tr, LANE), tab_map),
                  pl.BlockSpec((tr, LANE), tab_map),
                  pl.BlockSpec((1, HEAD_DIM), lambda i, j: (0, 0))],
        out_specs=pl.BlockSpec((tr, bw), lambda i, j: (i, j)),
        compiler_params=_cparams(("parallel", "parallel")),
        name="qk_prep",
    )(p, cos, sin, g)


def _rope_tables(seq, head_dim):
    t = np.arange(seq)
    rows = (t // GRID_W).astype(np.float32)
    cols = (t % GRID_W).astype(np.float32)
    quarter = head_dim // 4
    inv = (ROPE_THETA ** (-np.arange(quarter, dtype=np.float32) / quarter)).astype(np.float32)
    ang = np.concatenate([rows[:, None] * inv, cols[:, None] * inv], axis=-1)
    lane = np.arange(LANE)
    pair = (lane % head_dim) // 2
    sign = np.where(lane % 2 == 0, -1.0, 1.0).astype(np.float32)
    cos = np.cos(ang).astype(np.float32)[:, pair]
    sin = np.sin(ang).astype(np.float32)[:, pair] * sign
    return jnp.asarray(cos), jnp.asarray(sin)


def _qk(q, k):
    return lax.dot_general(q, k, (((1,), (1,)), ((), ())), preferred_element_type=F32)


def _stack_heads(q, n):
    if n == 1:
        return q
    return jnp.concatenate([q[:, g * LANE:(g + 1) * LANE] for g in range(n)], axis=0)


def _unstack_heads(o, n):
    if n == 1:
        return o
    t = o.shape[0] // n
    return jnp.concatenate([o[g * t:(g + 1) * t] for g in range(n)], axis=1)


def _stack_diff(q):
    lane = lax.broadcasted_iota(jnp.int32, q.shape, 1)
    zero = jnp.zeros_like(q)
    return jnp.concatenate([jnp.where(lane < DH_B, q, zero),
                            jnp.where(lane >= DH_B, q, zero)], axis=0)


def _softmax_attend(groups, sink=None):
    m = None
    for s, _ in groups:
        sm = jnp.max(s, axis=-1, keepdims=True)
        m = sm if m is None else jnp.maximum(m, sm)
    if sink is not None:
        m = jnp.maximum(m, sink)
    l = None
    acc = None
    for s, v in groups:
        p = jnp.exp(s - m)
        ps = jnp.sum(p, axis=-1, keepdims=True)
        pv = jnp.dot(p.astype(v.dtype), v, preferred_element_type=F32)
        l = ps if l is None else l + ps
        acc = pv if acc is None else acc + pv
    if sink is not None:
        l = l + jnp.exp(sink - m)
    return acc / l


def _diff_lambda(lam_ref, lam_init):
    a = jnp.sum(lam_ref[0:1, :] * lam_ref[1:2, :], axis=-1, keepdims=True)
    b = jnp.sum(lam_ref[2:3, :] * lam_ref[3:4, :], axis=-1, keepdims=True)
    return jnp.exp(a) - jnp.exp(b) + lam_init


def _diff_finalize(o, lam_ref, gsub_ref, lam_init):
    t = o.shape[0] // 2
    d = o[:t] - _diff_lambda(lam_ref, lam_init) * o[t:]
    return _rms(d, gsub_ref[...]) * (1.0 - lam_init)


def _sink_rows(sink_ref, kv, rows_per_head, n_rows):
    head = lax.broadcasted_iota(jnp.int32, (n_rows, 1), 0) // rows_per_head
    out = jnp.zeros((n_rows, 1), F32)
    for g in range(GQA_G):
        out = jnp.where(head == g, sink_ref[kv * GQA_G + g], out)
    return out


def _lane_tiles(s):
    return [s[:, j * LANE:(j + 1) * LANE] for j in range(s.shape[1] // LANE)]


def _flash_kernel(q_ref, kl_ref, vl_ref, kc_ref, vc_ref, cq_ref, sq_ref, ck_ref, sk_ref,
                  gq_ref, gk_ref, lam_ref, gsub_ref, o_ref, kp_ref, kcp_ref, *,
                  diff, n_stack, tk, lam_init, norm, q_scale, q_ready):
    gq = gq_ref[...] if norm else None
    gk = gk_ref[...] if norm else None

    @pl.when(pl.program_id(2) == 0)
    def _():
        for c in range(kl_ref.shape[0] // tk):
            sl = slice(c * tk, (c + 1) * tk)
            kp_ref[sl, :] = _prep_tile(kl_ref[sl, :], gk, ck_ref[sl, :], sk_ref[sl, :], 1.0)
        kcp_ref[...] = _prep_tile(kc_ref[...], gk, None, None, 1.0)

    kl_ref, kc_ref = kp_ref, kcp_ref
    if q_ready:
        q = q_ref[...]
    else:
        cq, sq = cq_ref[...], sq_ref[...]
        q = jnp.concatenate([_prep_tile(q_ref[:, g * LANE:(g + 1) * LANE], gq, cq, sq, q_scale)
                             for g in range(q_ref.shape[1] // LANE)], axis=1)
    q = _stack_diff(q) if diff else _stack_heads(q, n_stack)

    def row_max(tiles):
        m = tiles[0]
        for t in tiles[1:]:
            m = jnp.maximum(m, t)
        return jnp.broadcast_to(jnp.max(m, axis=-1, keepdims=True), m.shape)

    def weights(tiles, m, v):
        ps = [jnp.exp2(t - m) for t in tiles]
        l = ps[0]
        for p in ps[1:]:
            l = l + p
        pv = jnp.dot(jnp.concatenate(ps, axis=1).astype(v.dtype), v, preferred_element_type=F32)
        return l, pv

    def latent_scores(c):
        return _lane_tiles(_qk(q, kl_ref[c * tk:(c + 1) * tk, :]))

    n_chunks = kl_ref.shape[0] // tk
    tiles = _lane_tiles(_qk(q, kc_ref[...]))
    nxt = latent_scores(0)
    m = row_max(tiles)
    l, acc = weights(tiles, m, vc_ref[...])
    for c in range(n_chunks):
        tiles = nxt
        if c + 1 < n_chunks:
            nxt = latent_scores(c + 1)
        m_new = jnp.maximum(m, row_max(tiles))
        alpha = jnp.exp2(m - m_new)
        l_c, pv = weights(tiles, m_new, vl_ref[c * tk:(c + 1) * tk, :])
        l = alpha * l + l_c
        acc = alpha * acc + pv
        m = m_new
    o = acc / jnp.sum(l, axis=-1, keepdims=True)
    if diff:
        o = _diff_finalize(o, lam_ref, gsub_ref, lam_init)
    else:
        o = _unstack_heads(o, n_stack)
    o_ref[...] = o.astype(o_ref.dtype)


def _flash_call(p, pc, pc_off, q_col, k_col, v_col, rope, q_scale, *, batch, seq, ctx, diff,
                q_ready=None, g_q=None, g_k=None, lam_rows=None, g_sub=None, lam_init=0.0):
    n_heads = N_GROUP_HEADS if diff else KV_GQA
    n_stack = 2 if diff else GQA_G
    tq = min(512 // n_stack, seq)
    qw = LANE if diff else GQA_G * LANE
    qb = 0 if q_ready is not None else q_col * LANE // qw
    nq = seq // tq
    norm = g_k is not None
    ones = jnp.ones((HEAD_DIM,), F32)
    if lam_rows is None:
        lam_rows = jnp.zeros((8, LANE), F32)
    vec = lambda g: (ones if g is None else g).reshape(1, HEAD_DIM)
    cos, sin = rope
    kern = functools.partial(_flash_kernel, diff=diff, n_stack=n_stack, tk=min(1024, seq),
                             lam_init=float(lam_init), norm=norm, q_scale=float(q_scale),
                             q_ready=q_ready is not None)
    const = lambda b, h, i: (0, 0)
    return pl.pallas_call(
        kern,
        out_shape=jax.ShapeDtypeStruct((batch * seq, GROUP_W), BF16),
        grid=(batch, n_heads, nq),
        in_specs=[pl.BlockSpec((tq, qw), lambda b, h, i: (b * nq + i, qb + h)),
                  pl.BlockSpec((seq, LANE), lambda b, h, i: (b, k_col + h)),
                  pl.BlockSpec((seq, LANE), lambda b, h, i: (b, v_col + h)),
                  pl.BlockSpec((ctx, LANE), lambda b, h, i: (b, k_col - pc_off + h)),
                  pl.BlockSpec((ctx, LANE), lambda b, h, i: (b, v_col - pc_off + h)),
                  pl.BlockSpec((tq, LANE), lambda b, h, i: (i, 0)),
                  pl.BlockSpec((tq, LANE), lambda b, h, i: (i, 0)),
                  pl.BlockSpec((seq, LANE), const),
                  pl.BlockSpec((seq, LANE), const),
                  pl.BlockSpec((1, HEAD_DIM), const),
                  pl.BlockSpec((1, HEAD_DIM), const),
                  pl.BlockSpec((8, LANE), const),
                  pl.BlockSpec((1, HEAD_DIM), const)],
        out_specs=pl.BlockSpec((tq, qw), lambda b, h, i: (b * nq + i, h)),
        scratch_shapes=[pltpu.VMEM((seq, LANE), BF16), pltpu.VMEM((ctx, LANE), BF16)],
        compiler_params=_cparams(("parallel", "parallel", "arbitrary")),
        name="attn_full",
    )(p if q_ready is None else q_ready, p, p, pc, pc, cos, sin, cos, sin, vec(g_q), vec(g_k),
      lam_rows, vec(g_sub))


def _band_kernel(sink_ref, q_ref, kp_ref, kx_ref, kn_ref, vp_ref, vx_ref, vn_ref,
                 kc_ref, vc_ref, o_ref, *, tq, seq):
    kv = pl.program_id(1)
    i = pl.program_id(2)
    q = _stack_heads(q_ref[...], GQA_G)
    rows = q.shape[0]
    k_loc = jnp.concatenate([kp_ref[...], kx_ref[...], kn_ref[...]], axis=0)
    v_loc = jnp.concatenate([vp_ref[...], vx_ref[...], vn_ref[...]], axis=0)
    s_loc = _qk(q, k_loc)
    qpos = i * tq + (lax.broadcasted_iota(jnp.int32, s_loc.shape, 0) & (tq - 1))
    kpos = (i - 1) * tq + lax.broadcasted_iota(jnp.int32, s_loc.shape, 1)
    ok = jnp.where(kpos >= 0, jnp.abs(qpos - kpos), WIN_C + 1)
    ok = jnp.where(kpos < seq, ok, WIN_C + 1) <= WIN_C
    s_loc = jnp.where(ok, s_loc, NEG_INF)
    s_ctx = _qk(q, kc_ref[...])
    sink = _sink_rows(sink_ref, kv, tq, rows)
    o = _softmax_attend([(s_loc, v_loc), (s_ctx, vc_ref[...])], sink)
    o_ref[...] = _unstack_heads(o, GQA_G).astype(o_ref.dtype)


def _band_call(sink, q, kl, p, pc, pc_off, *, batch, seq, ctx):
    tq = WIN_C
    nq = seq // tq
    qw = GQA_G * LANE

    def nb(delta, col):
        def imap(b, h, i):
            return (b * nq + jnp.clip(i + delta, 0, nq - 1), col + h)
        return imap

    return pl.pallas_call(
        functools.partial(_band_kernel, tq=tq, seq=seq),
        out_shape=jax.ShapeDtypeStruct((batch * seq, GROUP_W), BF16),
        grid=(batch, KV_GQA, nq),
        in_specs=[pl.BlockSpec(memory_space=pltpu.SMEM),
                  pl.BlockSpec((tq, qw), lambda b, h, i: (b * nq + i, h)),
                  pl.BlockSpec((tq, LANE), nb(-1, 0)),
                  pl.BlockSpec((tq, LANE), nb(0, 0)),
                  pl.BlockSpec((tq, LANE), nb(1, 0)),
                  pl.BlockSpec((tq, LANE), nb(-1, VC_COL)),
                  pl.BlockSpec((tq, LANE), nb(0, VC_COL)),
                  pl.BlockSpec((tq, LANE), nb(1, VC_COL)),
                  pl.BlockSpec((ctx, LANE), lambda b, h, i: (b, KC_COL - pc_off + h)),
                  pl.BlockSpec((ctx, LANE), lambda b, h, i: (b, VC_COL - pc_off + h))],
        out_specs=pl.BlockSpec((tq, qw), lambda b, h, i: (b * nq + i, h)),
        compiler_params=_cparams(("parallel", "parallel", "parallel")),
        name="attn_band",
    )(sink, q, kl, kl, kl, p, p, p, pc, pc)


def _nbr_build_bias(tab_ref, bias_ref):
    shape = (GRID_W, LANE)
    w = lax.broadcasted_iota(jnp.int32, shape, 0)
    lane = lax.broadcasted_iota(jnp.int32, shape, 1)
    c = lane & (GRID_W - 1)
    cs = jnp.clip(w - NA_KW // 2, 0, GRID_W - NA_KW)
    d = c - cs
    mask = jnp.where(d >= 0, jnp.where(d < NA_KW, 0.0, NEG_INF), NEG_INF).astype(F32)
    neg = jnp.full(shape, NEG_INF, F32)
    toep = [pltpu.roll(jnp.broadcast_to(tab_ref[ro:ro + 1, :], shape), LANE - (NA_KW - 1), 1,
                       stride=1, stride_axis=0) + mask for ro in range(2 * NA_KH - 1)]
    kinds = (lambda dr: (dr, max(dr - NA_KH // 2, 0)),
             lambda dr: (NA_KH // 2 + dr, dr),
             lambda dr: (NBR_GROUP + dr, min(NA_KH // 2 + dr, NBR_SLAB - NA_KH)))
    for kind, rel in enumerate(kinds):
        for dr in range(NBR_GROUP):
            r_rel, rs_rel = rel(dr)
            for j in range(NBR_SLAB // 2):
                halves = [toep[kr - r_rel + NA_KH - 1] if rs_rel <= kr < rs_rel + NA_KH else neg
                          for kr in (2 * j, 2 * j + 1)]
                tile = neg if halves[0] is neg and halves[1] is neg else jnp.where(
                    lane < GRID_W, halves[0], halves[1])
                bias_ref[kind, dr * GRID_W:(dr + 1) * GRID_W, j * LANE:(j + 1) * LANE] = tile


def _nbr_kernel(q_ref, k_ref, v_ref, kc_ref, vc_ref, tab_ref, o_ref, bias_ref, *, n_rows, scale):
    n_groups = n_rows // NBR_GROUP
    gq = NBR_GROUP * GRID_W
    slab = NBR_SLAB * GRID_W
    kc = kc_ref[...]
    vc = vc_ref[...]
    _nbr_build_bias(tab_ref, bias_ref)

    def keys_of(g):
        base = min(max(g * NBR_GROUP - NA_KH // 2, 0), n_rows - NBR_SLAB)
        return slice(base * GRID_W, base * GRID_W + slab)

    def scores(g):
        q = q_ref[g * gq:(g + 1) * gq, :]
        return _qk(q, k_ref[keys_of(g), :]), _qk(q, kc)

    nxt = scores(0)
    for g in range(n_groups):
        raw_loc, raw_ctx = nxt
        if g + 1 < n_groups:
            nxt = scores(g + 1)
        kind = 0 if g == 0 else (2 if g == n_groups - 1 else 1)
        s_loc = raw_loc * scale + bias_ref[kind]
        o = _softmax_attend([(s_loc, v_ref[keys_of(g), :]), (raw_ctx * scale, vc)])
        o_ref[g * gq:(g + 1) * gq, :] = o.astype(o_ref.dtype)


def _nbr_table(na_bias):
    h, nr, nc = na_bias.shape
    t = jnp.pad(na_bias.astype(F32), ((0, 0), (0, 16 - nr), (0, GRID_W - nc)))
    return jnp.concatenate([t] * (LANE // GRID_W), axis=-1)


def _nbr_call(p, pc, pc_off, na_bias, *, batch, seq, ctx):
    n_rows = seq // GRID_W
    assert n_rows >= NBR_SLAB and n_rows % NBR_GROUP == 0 and 2 * GRID_W == LANE
    assert NBR_SLAB >= NBR_GROUP + NA_KH - 1 and NBR_SLAB == NBR_GROUP + NA_KH
    scale = HEAD_DIM ** -0.5
    return pl.pallas_call(
        functools.partial(_nbr_kernel, n_rows=n_rows, scale=scale),
        out_shape=jax.ShapeDtypeStruct((batch * seq, GROUP_W), BF16),
        grid=(batch, N_GROUP_HEADS),
        in_specs=[pl.BlockSpec((seq, LANE), lambda b, h: (b, QD_COL + h)),
                  pl.BlockSpec((seq, LANE), lambda b, h: (b, KD_COL + h)),
                  pl.BlockSpec((seq, LANE), lambda b, h: (b, VD_COL + h)),
                  pl.BlockSpec((ctx, LANE), lambda b, h: (b, KD_COL - pc_off + h)),
                  pl.BlockSpec((ctx, LANE), lambda b, h: (b, VD_COL - pc_off + h)),
                  pl.BlockSpec((None, 16, LANE), lambda b, h: (h, 0, 0))],
        out_specs=pl.BlockSpec((seq, LANE), lambda b, h: (b, h)),
        scratch_shapes=[pltpu.VMEM((3, NBR_GROUP * GRID_W, NBR_SLAB * GRID_W), F32)],
        compiler_params=_cparams(("parallel", "parallel")),
        name="attn_nbr",
    )(p, p, p, pc, pc, _nbr_table(na_bias))


def _ctx_kernel(sink_ref, q_ref, k_ref, v_ref, lam_ref, gsub_ref, o_ref, *,
                diff, n_stack, scale, use_sink, lam_init):
    q = q_ref[...]
    q = _stack_diff(q) if diff else _stack_heads(q, n_stack)
    s = _qk(q, k_ref[...])
    if scale != 1.0:
        s = s * scale
    sink = None
    if use_sink:
        sink = _sink_rows(sink_ref, pl.program_id(1), q_ref.shape[0], q.shape[0])
    o = _softmax_attend([(s, v_ref[...])], sink)
    if diff:
        o = _diff_finalize(o, lam_ref, gsub_ref, lam_init)
    else:
        o = _unstack_heads(o, n_stack)
    o_ref[...] = o.astype(o_ref.dtype)


def _ctx_call(q, q_col, k, k_col, v, v_col, *, batch, ctx, n_heads, n_stack, diff=False,
              scale=1.0, sink=None, lam_rows=None, g_sub=None, lam_init=0.0):
    qw = LANE if diff else n_stack * LANE
    qb = q_col * LANE // qw
    if lam_rows is None:
        lam_rows = jnp.zeros((8, LANE), F32)
        g_sub = jnp.ones((HEAD_DIM,), F32)
    use_sink = sink is not None
    if sink is None:
        sink = jnp.zeros((N_GROUP_HEADS,), F32)
    kern = functools.partial(_ctx_kernel, diff=diff, n_stack=n_stack, scale=float(scale),
                             use_sink=use_sink, lam_init=float(lam_init))
    return pl.pallas_call(
        kern,
        out_shape=jax.ShapeDtypeStruct((batch * ctx, GROUP_W), BF16),
        grid=(batch, n_heads),
        in_specs=[pl.BlockSpec(memory_space=pltpu.SMEM),
                  pl.BlockSpec((ctx, qw), lambda b, h: (b, qb + h)),
                  pl.BlockSpec((ctx, LANE), lambda b, h: (b, k_col + h)),
                  pl.BlockSpec((ctx, LANE), lambda b, h: (b, v_col + h)),
                  pl.BlockSpec((8, LANE), lambda b, h: (0, 0)),
                  pl.BlockSpec((1, HEAD_DIM), lambda b, h: (0, 0))],
        out_specs=pl.BlockSpec((ctx, qw), lambda b, h: (b, h)),
        compiler_params=_cparams(("parallel", "parallel")),
        name="attn_ctx",
    )(sink, q, k, v, lam_rows, g_sub.reshape(1, HEAD_DIM))


def _ffn_up_kernel(a_ref, ap_ref, an_ref, wg_ref, wv_ref, cwg_ref, cwv_ref, o_ref, ext_ref, *,
                   tiles_per_seq, row_axis):
    i = pl.program_id(row_axis)
    tm = a_ref.shape[0]

    @pl.when(jnp.logical_or(row_axis == 1, pl.program_id(1) == 0))
    def _():
        first = (i % tiles_per_seq) == 0
        last = (i % tiles_per_seq) == tiles_per_seq - 1
        zero = jnp.zeros(ap_ref.shape, ap_ref.dtype)
        ext_ref[0:HALO, :] = jnp.where(first, zero, ap_ref[...])
        ext_ref[HALO:HALO + tm, :] = a_ref[...]
        ext_ref[HALO + tm:, :] = jnp.where(last, zero, an_ref[...])

    a = ext_ref[...]
    rows = min(EPI_ROWS, tm)

    def conv(u, r0, cw_ref):
        w = u[HALO + r0 - 8:HALO + r0 + rows + 8]
        n = w.shape[0]
        cu = (pltpu.roll(w, 1, 0) * cw_ref[0:1, :] + w * cw_ref[1:2, :]
              + pltpu.roll(w, n - 1, 0) * cw_ref[2:3, :])
        return cu[8:8 + rows]

    ug = jnp.dot(a, wg_ref[...].astype(BF16), preferred_element_type=F32)
    uv = jnp.dot(a, wv_ref[...].astype(BF16), preferred_element_type=F32)
    for r0 in range(0, tm, rows):
        gate = conv(ug, r0, cwg_ref)
        val = conv(uv, r0, cwv_ref)
        o_ref[r0:r0 + rows, :] = (gate * jax.nn.sigmoid(gate) * val).astype(o_ref.dtype)


def _ffn_up_call(a, w_up, layer, conv_w, seq):
    m, kdim = a.shape
    f = w_up.shape[2] // 2
    tm = min(1024, seq)
    tn = _pick(f, (256, 128))
    tiles_per_seq = seq // tm
    fb = f // tn
    hb = tm // HALO
    n_halo = m // HALO
    cw = jnp.pad(conv_w, ((0, 8 - CONV_W), (0, 0)))
    row_axis = 1 if m // tm <= 2 else 0

    def spec(shape, fn):
        return pl.BlockSpec(shape, (lambda j, i: fn(i, j)) if row_axis == 1 else fn)

    return pl.pallas_call(
        functools.partial(_ffn_up_kernel, tiles_per_seq=tiles_per_seq, row_axis=row_axis),
        out_shape=jax.ShapeDtypeStruct((m, f), BF16),
        grid=(fb, m // tm) if row_axis == 1 else (m // tm, fb),
        in_specs=[spec((tm, kdim), lambda i, j: (i, 0)),
                  spec((HALO, kdim), lambda i, j: (jnp.maximum(i * hb - 1, 0), 0)),
                  spec((HALO, kdim), lambda i, j: (jnp.minimum((i + 1) * hb, n_halo - 1), 0)),
                  spec((None, kdim, tn), lambda i, j: (layer, 0, j)),
                  spec((None, kdim, tn), lambda i, j: (layer, 0, fb + j)),
                  spec((8, tn), lambda i, j: (0, j)),
                  spec((8, tn), lambda i, j: (0, fb + j))],
        out_specs=spec((tm, tn), lambda i, j: (i, j)),
        scratch_shapes=[pltpu.VMEM((tm + 2 * HALO, kdim), BF16)],
        compiler_params=_cparams(("parallel", "arbitrary")),
        name="ffn_up",
    )(a, a, a, w_up, w_up, cw, cw)


def _conv_ffn(hx2d, w_up, layer, conv_w, w_down, seq):
    g = _ffn_up_call(hx2d, w_up, layer, conv_w, seq)
    return _matmul_call(g, w_down, F32)


def kernel(x, c, ctx, c_ctx, w_ada, b_ada, g_attn_pre, g_attn_post, g_mlp_pre, g_mlp_post,
           w_in, w_out, g_q_a, g_k_a, lam_q1, lam_k1, lam_q2, lam_k2, g_sub_b, sink_c, na_bias,
           w_up, conv_ffn_w, w_down):
    batch, seq, d = x.shape
    n_ctx = ctx.shape[1]
    depth = w_ada.shape[0]
    assert batch + 1 <= 8 and d == N_Q

    cond = jnp.zeros((8, d), F32).at[:batch].set(c).at[batch].set(c_ctx)
    mod_all = _ada_call(cond, w_ada, b_ada)
    rope_hd = _rope_tables(seq, HEAD_DIM)
    rope_dh = _rope_tables(seq, DH_B)
    scale_hd = HEAD_DIM ** -0.5
    scale_dh = DH_B ** -0.5

    def latent_mod(l):
        mod = mod_all[l, :batch].reshape(batch, 6, d)
        return tuple(mod[:, k] for k in range(6))

    h = ctx
    sh_a, sc_a, gt_a, sh_m, sc_m, gt_m = latent_mod(0)
    hx = _norm_mod_call(x, g_attn_pre[0], sc_a, sh_a)
    for l in range(depth):
        last = l == depth - 1
        lam_init = 0.8 - 0.6 * math.exp(-0.3 * l)
        sh_a, sc_a, gt_a, sh_m, sc_m, gt_m = latent_mod(l)
        mod_c = mod_all[l, batch].reshape(6, d)
        one = lambda v: v.reshape(1, d)
        wd = w_down[l].astype(BF16)
        lam_rows = jnp.zeros((8, LANE), F32).at[:4, :DH_B].set(
            jnp.stack([lam_q1[l], lam_k1[l], lam_q2[l], lam_k2[l]]))

        hx = hx.reshape(batch * seq, d)
        hc = _norm_mod_call(h.reshape(1, batch * n_ctx, d), g_attn_pre[l],
                            one(mod_c[1]), one(mod_c[0])).reshape(batch * n_ctx, d)
        p = _matmul_ws_call(hx, w_in, l, BF16)
        pc_off = 32 if last else 0
        pc = _matmul_ws_call(hc, w_in, l, BF16, col0=pc_off * LANE)

        dims = dict(batch=batch, seq=seq, ctx=n_ctx)
        q_c = _prep_call(p, QC_COL, GROUP_W, seq, rope=rope_hd, scale=scale_hd)
        k_c = _prep_call(p, KC_COL, KV_GQA * LANE, seq, rope=rope_hd)
        q_a = _prep_call(p, QA_COL, GROUP_W, seq, gain=g_q_a[l], rope=rope_hd,
                         scale=scale_hd * LOG2E)
        o_a = _flash_call(p, pc, pc_off, QA_COL, KA_COL, VA_COL, rope_hd, 1.0, diff=False,
                          q_ready=q_a, g_k=g_k_a[l], **dims)
        o_b = _flash_call(p, pc, pc_off, QB_COL, KB_COL, VB_COL, rope_dh, scale_dh * LOG2E,
                          diff=True, lam_rows=lam_rows, g_sub=g_sub_b[l], lam_init=lam_init, **dims)
        o_c = _band_call(sink_c[l], q_c, k_c, p, pc, pc_off, **dims)
        o_d = _nbr_call(p, pc, pc_off, na_bias[l], **dims)
        o = jnp.concatenate([o_a, o_b, o_c, o_d], axis=-1)
        y = _matmul_ws_call(o, w_out, l, F32)
        x, hx = _norm_res_mod_call(x, y, g_attn_post[l], gt_a, g_mlp_pre[l], sc_m, sh_m)

        y = _conv_ffn(hx.reshape(batch * seq, d), w_up, l, conv_ffn_w[l], wd, seq)
        if last:
            x = _norm_res_call(x, y, g_mlp_post[l], gt_m)
        else:
            nxt = latent_mod(l + 1)
            x, hx = _norm_res_mod_call(x, y, g_mlp_post[l], gt_m, g_attn_pre[l + 1], nxt[1], nxt[0])

        if not last:
            cdims = dict(batch=batch, ctx=n_ctx)
            qc_a = _prep_call(pc, QA_COL, GROUP_W, n_ctx, gain=g_q_a[l], scale=scale_hd)
            kc_a = _prep_call(pc, KA_COL, KV_GQA * LANE, n_ctx, gain=g_k_a[l])
            co_a = _ctx_call(qc_a, 0, kc_a, 0, pc, VA_COL, n_heads=KV_GQA, n_stack=GQA_G, **cdims)
            co_b = _ctx_call(pc, QB_COL, pc, KB_COL, pc, VB_COL, n_heads=N_GROUP_HEADS,
                             n_stack=2, diff=True, scale=scale_dh, lam_rows=lam_rows,
                             g_sub=g_sub_b[l], lam_init=lam_init, **cdims)
            co_c = _ctx_call(pc, QC_COL, pc, KC_COL, pc, VC_COL, n_heads=KV_GQA, n_stack=GQA_G,
                             scale=scale_hd, sink=sink_c[l], **cdims)
            co_d = _ctx_call(pc, QD_COL, pc, KD_COL, pc, VD_COL, n_heads=N_GROUP_HEADS,
                             n_stack=1, scale=scale_hd, **cdims)
            co = jnp.concatenate([co_a, co_b, co_c, co_d], axis=-1)
            yc = _matmul_ws_call(co, w_out, l, F32)
            h3 = h.reshape(1, batch * n_ctx, d)
            h3 = _norm_res_call(h3, yc, g_attn_post[l], one(mod_c[2]))
            hc = _norm_mod_call(h3, g_mlp_pre[l], one(mod_c[4]), one(mod_c[3]))
            yc = _conv_ffn(hc.reshape(batch * n_ctx, d), w_up, l, conv_ffn_w[l], wd, n_ctx)
            h = _norm_res_call(h3, yc, g_mlp_post[l], one(mod_c[5])).reshape(batch, n_ctx, d)
    return x
```

```python
import functools
import math

import jax
import jax.numpy as jnp
import numpy as np
from jax import lax
from jax.experimental import pallas as pl
from jax.experimental.pallas import tpu as pltpu

F32 = jnp.float32
BF16 = jnp.bfloat16

GRID_W = 64
HEAD_DIM = 128
N_GROUP_HEADS = 8
KV_GQA = 2
GQA_G = N_GROUP_HEADS // KV_GQA
DH_B = HEAD_DIM // 2
WIN_C = 128
NA_KH = 8
NA_KW = 16
NBR_GROUP = 8
NBR_SLAB = 16
CONV_W = 3
ROPE_THETA = 10000.0
EPS = 1e-6
NEG_INF = -1e30
LOG2E = math.log2(math.e)
GROUP_W = N_GROUP_HEADS * HEAD_DIM
N_Q = 4 * GROUP_W

QA_COL, QB_COL, QC_COL, QD_COL = 0, 8, 16, 24
KA_COL, VA_COL = 32, 34
KB_COL, VB_COL = 36, 44
KC_COL, VC_COL = 52, 54
KD_COL, VD_COL = 56, 64
KV_COLS = 40

LANE = 128
HALO = 16
EPI_ROWS = 128
VMEM_LIMIT = 56 * 1024 * 1024


def _cparams(sem):
    return pltpu.CompilerParams(dimension_semantics=sem, vmem_limit_bytes=VMEM_LIMIT)


def _ada_kernel(a_ref, w_ref, b_ref, o_ref):
    a = a_ref[...]
    a = a * jax.nn.sigmoid(a)
    o_ref[...] = jnp.dot(a.astype(BF16), w_ref[...].astype(BF16),
                         preferred_element_type=F32) + b_ref[...]


def _ada_call(cond, w_ada, b_ada):
    n_layers, d, n = w_ada.shape
    tn = 512
    return pl.pallas_call(
        _ada_kernel,
        out_shape=jax.ShapeDtypeStruct((n_layers, 8, n), F32),
        grid=(n_layers, n // tn),
        in_specs=[pl.BlockSpec((8, d), lambda l, j: (0, 0)),
                  pl.BlockSpec((None, d, tn), lambda l, j: (l, 0, j)),
                  pl.BlockSpec((None, 1, tn), lambda l, j: (l, 0, j))],
        out_specs=pl.BlockSpec((None, 8, tn), lambda l, j: (l, 0, j)),
        compiler_params=_cparams(("parallel", "parallel")),
        name="ada_mod",
    )(cond, w_ada, b_ada.reshape(n_layers, 1, n))


def _rms(x, g):
    return x * lax.rsqrt(jnp.mean(x * x, axis=-1, keepdims=True) + EPS) * g


def _norm_mod_kernel(x_ref, g_ref, sc_ref, sh_ref, o_ref):
    y = _rms(x_ref[...], g_ref[...])
    o_ref[...] = (y * (1.0 + sc_ref[...]) + sh_ref[...]).astype(o_ref.dtype)


def _norm_mod_call(x, g, sc, sh):
    nb, r, d = x.shape
    tr = min(256, r)
    return pl.pallas_call(
        _norm_mod_kernel,
        out_shape=jax.ShapeDtypeStruct((nb, r, d), BF16),
        grid=(nb, r // tr),
        in_specs=[pl.BlockSpec((None, tr, d), lambda b, i: (b, i, 0)),
                  pl.BlockSpec((1, d), lambda b, i: (0, 0)),
                  pl.BlockSpec((None, 1, d), lambda b, i: (b, 0, 0)),
                  pl.BlockSpec((None, 1, d), lambda b, i: (b, 0, 0))],
        out_specs=pl.BlockSpec((None, tr, d), lambda b, i: (b, i, 0)),
        compiler_params=_cparams(("parallel", "parallel")),
        name="norm_mod",
    )(x, g.reshape(1, d), sc.reshape(nb, 1, d), sh.reshape(nb, 1, d))


def _norm_res_kernel(x_ref, y_ref, g_ref, gt_ref, o_ref):
    o_ref[...] = x_ref[...] + gt_ref[...] * _rms(y_ref[...].astype(F32), g_ref[...])


def _norm_res_call(x, y, g, gate):
    nb, r, d = x.shape
    tr = min(256, r)
    return pl.pallas_call(
        _norm_res_kernel,
        out_shape=jax.ShapeDtypeStruct((nb, r, d), F32),
        grid=(nb, r // tr),
        in_specs=[pl.BlockSpec((None, tr, d), lambda b, i: (b, i, 0)),
                  pl.BlockSpec((None, tr, d), lambda b, i: (b, i, 0)),
                  pl.BlockSpec((1, d), lambda b, i: (0, 0)),
                  pl.BlockSpec((None, 1, d), lambda b, i: (b, 0, 0))],
        out_specs=pl.BlockSpec((None, tr, d), lambda b, i: (b, i, 0)),
        compiler_params=_cparams(("parallel", "parallel")),
        name="norm_res",
    )(x, y.reshape(nb, r, d), g.reshape(1, d), gate.reshape(nb, 1, d))


def _norm_res_mod_kernel(x_ref, y_ref, g_ref, gt_ref, g2_ref, sc_ref, sh_ref, o_ref, h_ref):
    x = x_ref[...] + gt_ref[...] * _rms(y_ref[...].astype(F32), g_ref[...])
    o_ref[...] = x
    h_ref[...] = (_rms(x, g2_ref[...]) * (1.0 + sc_ref[...]) + sh_ref[...]).astype(h_ref.dtype)


def _norm_res_mod_call(x, y, g, gate, g_next, sc, sh):
    nb, r, d = x.shape
    tr = min(256, r)
    row = pl.BlockSpec((None, tr, d), lambda b, i: (b, i, 0))
    vec = pl.BlockSpec((None, 1, d), lambda b, i: (b, 0, 0))
    gain = pl.BlockSpec((1, d), lambda b, i: (0, 0))
    return pl.pallas_call(
        _norm_res_mod_kernel,
        out_shape=(jax.ShapeDtypeStruct((nb, r, d), F32), jax.ShapeDtypeStruct((nb, r, d), BF16)),
        grid=(nb, r // tr),
        in_specs=[row, row, gain, vec, gain, vec, vec],
        out_specs=(row, row),
        compiler_params=_cparams(("parallel", "parallel")),
        name="norm_res_mod",
    )(x, y.reshape(nb, r, d), g.reshape(1, d), gate.reshape(nb, 1, d), g_next.reshape(1, d),
      sc.reshape(nb, 1, d), sh.reshape(nb, 1, d))


def _matmul_kernel(a_ref, w_ref, o_ref, acc_ref, *, nk):
    k = pl.program_id(2)
    part = jnp.dot(a_ref[...], w_ref[...], preferred_element_type=F32)
    if nk == 1:
        o_ref[...] = part.astype(o_ref.dtype)
    else:
        @pl.when(k == 0)
        def _():
            acc_ref[...] = part

        @pl.when(jnp.logical_and(k > 0, k < nk - 1))
        def _():
            acc_ref[...] += part

        @pl.when(k == nk - 1)
        def _():
            o_ref[...] = (acc_ref[...] + part).astype(o_ref.dtype)


def _pick(n, prefs):
    for t in prefs:
        if n % t == 0:
            return t
    return n


def _matmul_call(a, w, out_dtype):
    m, kdim = a.shape
    _, n = w.shape
    tm = _pick(m, (1024, 512, 256))
    tn = _pick(n, (512, 256, 128))
    tk = kdim if kdim <= 4096 else _pick(kdim, (5504, 4096, 2048))
    nk = kdim // tk
    return pl.pallas_call(
        functools.partial(_matmul_kernel, nk=nk),
        out_shape=jax.ShapeDtypeStruct((m, n), out_dtype),
        grid=(m // tm, n // tn, nk),
        in_specs=[pl.BlockSpec((tm, tk), lambda i, j, k: (i, k)),
                  pl.BlockSpec((tk, tn), lambda i, j, k: (k, j))],
        out_specs=pl.BlockSpec((tm, tn), lambda i, j, k: (i, j)),
        scratch_shapes=[pltpu.VMEM((tm, tn) if nk > 1 else (8, LANE), F32)],
        compiler_params=_cparams(("parallel", "parallel", "arbitrary")),
        name="matmul",
    )(a, w)


def _matmul_ws_kernel(a_ref, w_ref, o_ref, wb_ref):
    @pl.when(pl.program_id(1) == 0)
    def _():
        wb_ref[...] = w_ref[...].astype(BF16)

    o_ref[...] = jnp.dot(a_ref[...], wb_ref[...], preferred_element_type=F32).astype(o_ref.dtype)


def _matmul_ws_call(a, w, layer, out_dtype, col0=0):
    m, kdim = a.shape
    n = w.shape[2] - col0
    tm = _pick(m, (1024, 512, 256))
    tn = _pick(n, (512, 256, 128))
    assert col0 % tn == 0
    cb0 = col0 // tn
    return pl.pallas_call(
        _matmul_ws_kernel,
        out_shape=jax.ShapeDtypeStruct((m, n), out_dtype),
        grid=(n // tn, m // tm),
        in_specs=[pl.BlockSpec((tm, kdim), lambda j, i: (i, 0)),
                  pl.BlockSpec((None, kdim, tn), lambda j, i: (layer, 0, cb0 + j))],
        out_specs=pl.BlockSpec((tm, tn), lambda j, i: (i, j)),
        scratch_shapes=[pltpu.VMEM((kdim, tn), BF16)],
        compiler_params=_cparams(("parallel", "arbitrary")),
        name="matmul_ws",
    )(a, w)


def _swap_pairs(y):
    lane = lax.broadcasted_iota(jnp.int32, y.shape, 1)
    nxt = pltpu.roll(y, LANE - 1, 1)
    prv = pltpu.roll(y, 1, 1)
    return jnp.where((lane & 1) == 0, nxt, prv)


def _prep_tile(y, gain, cos, sin, scale):
    y = y.astype(F32)
    if gain is not None:
        y = _rms(y, gain)
    if cos is not None:
        y = y * cos + _swap_pairs(y) * sin
    if scale != 1.0:
        y = y * scale
    return y.astype(BF16)


def _prep_kernel(p_ref, cos_ref, sin_ref, g_ref, o_ref, *, n_chunks, norm, rope, scale):
    for c in range(n_chunks):
        sl = slice(c * LANE, (c + 1) * LANE)
        o_ref[:, sl] = _prep_tile(p_ref[:, sl], g_ref[...] if norm else None,
                                  cos_ref[...] if rope else None,
                                  sin_ref[...] if rope else None, scale)


def _prep_call(p, col0, width, seq, *, gain=None, rope=None, scale=1.0):
    rows = p.shape[0]
    tr = min(256, seq)
    n_seq_tiles = seq // tr
    bw = min(width, 512)
    assert (col0 * LANE) % bw == 0 and width % bw == 0
    cb0 = col0 * LANE // bw
    if rope is None:
        cos = sin = jnp.zeros((tr, LANE), F32)
        tab_map = lambda i, j: (0, 0)
    else:
        cos, sin = rope
        tab_map = lambda i, j: (i % n_seq_tiles, 0)
    g = (jnp.ones((HEAD_DIM,), F32) if gain is None else gain).reshape(1, HEAD_DIM)
    kern = functools.partial(_prep_kernel, n_chunks=bw // LANE, norm=gain is not None,
                             rope=rope is not None, scale=float(scale))
    return pl.pallas_call(
        kern,
        out_shape=jax.ShapeDtypeStruct((rows, width), BF16),
        grid=(rows // tr, width // bw),
        in_specs=[pl.BlockSpec((tr, bw), lambda i, j: (i, cb0 + j)),
                  pl.BlockSpec((tr, LANE), tab_map),
                  pl.BlockSpec((tr, LANE), tab_map),
                  pl.BlockSpec((1, HEAD_DIM), lambda i, j: (0, 0))],
        out_specs=pl.BlockSpec((tr, bw), lambda i, j: (i, j)),
        compiler_params=_cparams(("parallel", "parallel")),
        name="qk_prep",
    )(p, cos, sin, g)


def _rope_tables(seq, head_dim):
    t = np.arange(seq)
    rows = (t // GRID_W).astype(np.float32)
    cols = (t % GRID_W).astype(np.float32)
    quarter = head_dim // 4
    inv = (ROPE_THETA ** (-np.arange(quarter, dtype=np.float32) / quarter)).astype(np.float32)
    ang = np.concatenate([rows[:, None] * inv, cols[:, None] * inv], axis=-1)
    lane = np.arange(LANE)
    pair = (lane % head_dim) // 2
    sign = np.where(lane % 2 == 0, -1.0, 1.0).astype(np.float32)
    cos = np.cos(ang).astype(np.float32)[:, pair]
    sin = np.sin(ang).astype(np.float32)[:, pair] * sign
    return jnp.asarray(cos), jnp.asarray(sin)


def _qk(q, k):
    return lax.dot_general(q, k, (((1,), (1,)), ((), ())), preferred_element_type=F32)


def _stack_heads(q, n):
    if n == 1:
        return q
    return jnp.concatenate([q[:, g * LANE:(g + 1) * LANE] for g in range(n)], axis=0)


def _unstack_heads(o, n):
    if n == 1:
        return o
    t = o.shape[0] // n
    return jnp.concatenate([o[g * t:(g + 1) * t] for g in range(n)], axis=1)


def _stack_diff(q):
    lane = lax.broadcasted_iota(jnp.int32, q.shape, 1)
    zero = jnp.zeros_like(q)
    return jnp.concatenate([jnp.where(lane < DH_B, q, zero),
                            jnp.where(lane >= DH_B, q, zero)], axis=0)


def _softmax_attend(groups, sink=None):
    m = None
    for s, _ in groups:
        sm = jnp.max(s, axis=-1, keepdims=True)
        m = sm if m is None else jnp.maximum(m, sm)
    if sink is not None:
        m = jnp.maximum(m, sink)
    l = None
    acc = None
    for s, v in groups:
        p = jnp.exp(s - m)
        ps = jnp.sum(p, axis=-1, keepdims=True)
        pv = jnp.dot(p.astype(v.dtype), v, preferred_element_type=F32)
        l = ps if l is None else l + ps
        acc = pv if acc is None else acc + pv
    if sink is not None:
        l = l + jnp.exp(sink - m)
    return acc / l


def _diff_lambda(lam_ref, lam_init):
    a = jnp.sum(lam_ref[0:1, :] * lam_ref[1:2, :], axis=-1, keepdims=True)
    b = jnp.sum(lam_ref[2:3, :] * lam_ref[3:4, :], axis=-1, keepdims=True)
    return jnp.exp(a) - jnp.exp(b) + lam_init


def _diff_finalize(o, lam_ref, gsub_ref, lam_init):
    t = o.shape[0] // 2
    d = o[:t] - _diff_lambda(lam_ref, lam_init) * o[t:]
    return _rms(d, gsub_ref[...]) * (1.0 - lam_init)


def _sink_rows(sink_ref, kv, rows_per_head, n_rows):
    head = lax.broadcasted_iota(jnp.int32, (n_rows, 1), 0) // rows_per_head
    out = jnp.zeros((n_rows, 1), F32)
    for g in range(GQA_G):
        out = jnp.where(head == g, sink_ref[kv * GQA_G + g], out)
    return out


def _lane_tiles(s):
    return [s[:, j * LANE:(j + 1) * LANE] for j in range(s.shape[1] // LANE)]


def _flash_kernel(q_ref, kl_ref, vl_ref, kc_ref, vc_ref, cq_ref, sq_ref, ck_ref, sk_ref,
                  gq_ref, gk_ref, lam_ref, gsub_ref, o_ref, kp_ref, kcp_ref, *,
                  diff, n_stack, tk, lam_init, norm, q_scale, q_ready):
    gq = gq_ref[...] if norm else None
    gk = gk_ref[...] if norm else None

    @pl.when(pl.program_id(2) == 0)
    def _():
        for c in range(kl_ref.shape[0] // tk):
            sl = slice(c * tk, (c + 1) * tk)
            kp_ref[sl, :] = _prep_tile(kl_ref[sl, :], gk, ck_ref[sl, :], sk_ref[sl, :], 1.0)
        kcp_ref[...] = _prep_tile(kc_ref[...], gk, None, None, 1.0)

    kl_ref, kc_ref = kp_ref, kcp_ref
    if q_ready:
        q = q_ref[...]
    else:
        cq, sq = cq_ref[...], sq_ref[...]
        q = jnp.concatenate([_prep_tile(q_ref[:, g * LANE:(g + 1) * LANE], gq, cq, sq, q_scale)
                             for g in range(q_ref.shape[1] // LANE)], axis=1)
    q = _stack_diff(q) if diff else _stack_heads(q, n_stack)

    def row_max(tiles):
        m = tiles[0]
        for t in tiles[1:]:
            m = jnp.maximum(m, t)
        return jnp.broadcast_to(jnp.max(m, axis=-1, keepdims=True), m.shape)

    def weights(tiles, m, v):
        ps = [jnp.exp2(t - m) for t in tiles]
        l = ps[0]
        for p in ps[1:]:
            l = l + p
        pv = jnp.dot(jnp.concatenate(ps, axis=1).astype(v.dtype), v, preferred_element_type=F32)
        return l, pv

    def latent_scores(c):
        return _lane_tiles(_qk(q, kl_ref[c * tk:(c + 1) * tk, :]))

    n_chunks = kl_ref.shape[0] // tk
    tiles = _lane_tiles(_qk(q, kc_ref[...]))
    nxt = latent_scores(0)
    m = row_max(tiles)
    l, acc = weights(tiles, m, vc_ref[...])
    for c in range(n_chunks):
        tiles = nxt
        if c + 1 < n_chunks:
            nxt = latent_scores(c + 1)
        m_new = jnp.maximum(m, row_max(tiles))
        alpha = jnp.exp2(m - m_new)
        l_c, pv = weights(tiles, m_new, vl_ref[c * tk:(c + 1) * tk, :])
        l = alpha * l + l_c
        acc = alpha * acc + pv
        m = m_new
    o = acc / jnp.sum(l, axis=-1, keepdims=True)
    if diff:
        o = _diff_finalize(o, lam_ref, gsub_ref, lam_init)
    else:
        o = _unstack_heads(o, n_stack)
    o_ref[...] = o.astype(o_ref.dtype)


def _flash_call(p, pc, pc_off, q_col, k_col, v_col, rope, q_scale, *, batch, seq, ctx, diff,
                q_ready=None, g_q=None, g_k=None, lam_rows=None, g_sub=None, lam_init=0.0):
    n_heads = N_GROUP_HEADS if diff else KV_GQA
    n_stack = 2 if diff else GQA_G
    tq = min(1024 // n_stack, seq)
    qw = LANE if diff else GQA_G * LANE
    qb = 0 if q_ready is not None else q_col * LANE // qw
    nq = seq // tq
    norm = g_k is not None
    ones = jnp.ones((HEAD_DIM,), F32)
    if lam_rows is None:
        lam_rows = jnp.zeros((8, LANE), F32)
    vec = lambda g: (ones if g is None else g).reshape(1, HEAD_DIM)
    cos, sin = rope
    kern = functools.partial(_flash_kernel, diff=diff, n_stack=n_stack, tk=min(1024, seq),
                             lam_init=float(lam_init), norm=norm, q_scale=float(q_scale),
                             q_ready=q_ready is not None)
    const = lambda b, h, i: (0, 0)
    return pl.pallas_call(
        kern,
        out_shape=jax.ShapeDtypeStruct((batch * seq, GROUP_W), BF16),
        grid=(batch, n_heads, nq),
        in_specs=[pl.BlockSpec((tq, qw), lambda b, h, i: (b * nq + i, qb + h)),
                  pl.BlockSpec((seq, LANE), lambda b, h, i: (b, k_col + h)),
                  pl.BlockSpec((seq, LANE), lambda b, h, i: (b, v_col + h)),
                  pl.BlockSpec((ctx, LANE), lambda b, h, i: (b, k_col - pc_off + h)),
                  pl.BlockSpec((ctx, LANE), lambda b, h, i: (b, v_col - pc_off + h)),
                  pl.BlockSpec((tq, LANE), lambda b, h, i: (i, 0)),
                  pl.BlockSpec((tq, LANE), lambda b, h, i: (i, 0)),
                  pl.BlockSpec((seq, LANE), const),
                  pl.BlockSpec((seq, LANE), const),
                  pl.BlockSpec((1, HEAD_DIM), const),
                  pl.BlockSpec((1, HEAD_DIM), const),
                  pl.BlockSpec((8, LANE), const),
                  pl.BlockSpec((1, HEAD_DIM), const)],
        out_specs=pl.BlockSpec((tq, qw), lambda b, h, i: (b * nq + i, h)),
        scratch_shapes=[pltpu.VMEM((seq, LANE), BF16), pltpu.VMEM((ctx, LANE), BF16)],
        compiler_params=_cparams(("parallel", "parallel", "arbitrary")),
        name="attn_full",
    )(p if q_ready is None else q_ready, p, p, pc, pc, cos, sin, cos, sin, vec(g_q), vec(g_k),
      lam_rows, vec(g_sub))


def _band_kernel(sink_ref, q_ref, kp_ref, kx_ref, kn_ref, vp_ref, vx_ref, vn_ref,
                 kc_ref, vc_ref, o_ref, *, tq, seq):
    kv = pl.program_id(1)
    i = pl.program_id(2)
    q = _stack_heads(q_ref[...], GQA_G)
    rows = q.shape[0]
    k_loc = jnp.concatenate([kp_ref[...], kx_ref[...], kn_ref[...]], axis=0)
    v_loc = jnp.concatenate([vp_ref[...], vx_ref[...], vn_ref[...]], axis=0)
    s_loc = _qk(q, k_loc)
    qpos = i * tq + (lax.broadcasted_iota(jnp.int32, s_loc.shape, 0) & (tq - 1))
    kpos = (i - 1) * tq + lax.broadcasted_iota(jnp.int32, s_loc.shape, 1)
    ok = jnp.where(kpos >= 0, jnp.abs(qpos - kpos), WIN_C + 1)
    ok = jnp.where(kpos < seq, ok, WIN_C + 1) <= WIN_C
    s_loc = jnp.where(ok, s_loc, NEG_INF)
    s_ctx = _qk(q, kc_ref[...])
    sink = _sink_rows(sink_ref, kv, tq, rows)
    o = _softmax_attend([(s_loc, v_loc), (s_ctx, vc_ref[...])], sink)
    o_ref[...] = _unstack_heads(o, GQA_G).astype(o_ref.dtype)


def _band_call(sink, q, kl, p, pc, pc_off, *, batch, seq, ctx):
    tq = WIN_C
    nq = seq // tq
    qw = GQA_G * LANE

    def nb(delta, col):
        def imap(b, h, i):
            return (b * nq + jnp.clip(i + delta, 0, nq - 1), col + h)
        return imap

    return pl.pallas_call(
        functools.partial(_band_kernel, tq=tq, seq=seq),
        out_shape=jax.ShapeDtypeStruct((batch * seq, GROUP_W), BF16),
        grid=(batch, KV_GQA, nq),
        in_specs=[pl.BlockSpec(memory_space=pltpu.SMEM),
                  pl.BlockSpec((tq, qw), lambda b, h, i: (b * nq + i, h)),
                  pl.BlockSpec((tq, LANE), nb(-1, 0)),
                  pl.BlockSpec((tq, LANE), nb(0, 0)),
                  pl.BlockSpec((tq, LANE), nb(1, 0)),
                  pl.BlockSpec((tq, LANE), nb(-1, VC_COL)),
                  pl.BlockSpec((tq, LANE), nb(0, VC_COL)),
                  pl.BlockSpec((tq, LANE), nb(1, VC_COL)),
                  pl.BlockSpec((ctx, LANE), lambda b, h, i: (b, KC_COL - pc_off + h)),
                  pl.BlockSpec((ctx, LANE), lambda b, h, i: (b, VC_COL - pc_off + h))],
        out_specs=pl.BlockSpec((tq, qw), lambda b, h, i: (b * nq + i, h)),
        compiler_params=_cparams(("parallel", "parallel", "parallel")),
        name="attn_band",
    )(sink, q, kl, kl, kl, p, p, p, pc, pc)


def _nbr_build_bias(tab_ref, bias_ref):
    shape = (GRID_W, LANE)
    w = lax.broadcasted_iota(jnp.int32, shape, 0)
    lane = lax.broadcasted_iota(jnp.int32, shape, 1)
    c = lane & (GRID_W - 1)
    cs = jnp.clip(w - NA_KW // 2, 0, GRID_W - NA_KW)
    d = c - cs
    mask = jnp.where(d >= 0, jnp.where(d < NA_KW, 0.0, NEG_INF), NEG_INF).astype(F32)
    neg = jnp.full(shape, NEG_INF, F32)
    toep = [pltpu.roll(jnp.broadcast_to(tab_ref[ro:ro + 1, :], shape), LANE - (NA_KW - 1), 1,
                       stride=1, stride_axis=0) + mask for ro in range(2 * NA_KH - 1)]
    kinds = (lambda dr: (dr, max(dr - NA_KH // 2, 0)),
             lambda dr: (NA_KH // 2 + dr, dr),
             lambda dr: (NBR_GROUP + dr, min(NA_KH // 2 + dr, NBR_SLAB - NA_KH)))
    for kind, rel in enumerate(kinds):
        for dr in range(NBR_GROUP):
            r_rel, rs_rel = rel(dr)
            for j in range(NBR_SLAB // 2):
                halves = [toep[kr - r_rel + NA_KH - 1] if rs_rel <= kr < rs_rel + NA_KH else neg
                          for kr in (2 * j, 2 * j + 1)]
                tile = neg if halves[0] is neg and halves[1] is neg else jnp.where(
                    lane < GRID_W, halves[0], halves[1])
                bias_ref[kind, dr * GRID_W:(dr + 1) * GRID_W, j * LANE:(j + 1) * LANE] = tile


def _nbr_kernel(q_ref, k_ref, v_ref, kc_ref, vc_ref, tab_ref, o_ref, bias_ref, *, n_rows, scale):
    n_groups = n_rows // NBR_GROUP
    gq = NBR_GROUP * GRID_W
    slab = NBR_SLAB * GRID_W
    kc = kc_ref[...]
    vc = vc_ref[...]
    _nbr_build_bias(tab_ref, bias_ref)

    def keys_of(g):
        base = min(max(g * NBR_GROUP - NA_KH // 2, 0), n_rows - NBR_SLAB)
        return slice(base * GRID_W, base * GRID_W + slab)

    def scores(g):
        q = q_ref[g * gq:(g + 1) * gq, :]
        return _qk(q, k_ref[keys_of(g), :]), _qk(q, kc)

    nxt = scores(0)
    for g in range(n_groups):
        raw_loc, raw_ctx = nxt
        if g + 1 < n_groups:
            nxt = scores(g + 1)
        kind = 0 if g == 0 else (2 if g == n_groups - 1 else 1)
        s_loc = raw_loc * scale + bias_ref[kind]
        o = _softmax_attend([(s_loc, v_ref[keys_of(g), :]), (raw_ctx * scale, vc)])
        o_ref[g * gq:(g + 1) * gq, :] = o.astype(o_ref.dtype)


def _nbr_table(na_bias):
    h, nr, nc = na_bias.shape
    t = jnp.pad(na_bias.astype(F32), ((0, 0), (0, 16 - nr), (0, GRID_W - nc)))
    return jnp.concatenate([t] * (LANE // GRID_W), axis=-1)


def _nbr_call(p, pc, pc_off, na_bias, *, batch, seq, ctx):
    n_rows = seq // GRID_W
    assert n_rows >= NBR_SLAB and n_rows % NBR_GROUP == 0 and 2 * GRID_W == LANE
    assert NBR_SLAB >= NBR_GROUP + NA_KH - 1 and NBR_SLAB == NBR_GROUP + NA_KH
    scale = HEAD_DIM ** -0.5
    return pl.pallas_call(
        functools.partial(_nbr_kernel, n_rows=n_rows, scale=scale),
        out_shape=jax.ShapeDtypeStruct((batch * seq, GROUP_W), BF16),
        grid=(batch, N_GROUP_HEADS),
        in_specs=[pl.BlockSpec((seq, LANE), lambda b, h: (b, QD_COL + h)),
                  pl.BlockSpec((seq, LANE), lambda b, h: (b, KD_COL + h)),
                  pl.BlockSpec((seq, LANE), lambda b, h: (b, VD_COL + h)),
                  pl.BlockSpec((ctx, LANE), lambda b, h: (b, KD_COL - pc_off + h)),
                  pl.BlockSpec((ctx, LANE), lambda b, h: (b, VD_COL - pc_off + h)),
                  pl.BlockSpec((None, 16, LANE), lambda b, h: (h, 0, 0))],
        out_specs=pl.BlockSpec((seq, LANE), lambda b, h: (b, h)),
        scratch_shapes=[pltpu.VMEM((3, NBR_GROUP * GRID_W, NBR_SLAB * GRID_W), F32)],
        compiler_params=_cparams(("parallel", "parallel")),
        name="attn_nbr",
    )(p, p, p, pc, pc, _nbr_table(na_bias))


def _ctx_kernel(sink_ref, q_ref, k_ref, v_ref, lam_ref, gsub_ref, o_ref, *,
                diff, n_stack, scale, use_sink, lam_init):
    q = q_ref[...]
    q = _stack_diff(q) if diff else _stack_heads(q, n_stack)
    s = _qk(q, k_ref[...])
    if scale != 1.0:
        s = s * scale
    sink = None
    if use_sink:
        sink = _sink_rows(sink_ref, pl.program_id(1), q_ref.shape[0], q.shape[0])
    o = _softmax_attend([(s, v_ref[...])], sink)
    if diff:
        o = _diff_finalize(o, lam_ref, gsub_ref, lam_init)
    else:
        o = _unstack_heads(o, n_stack)
    o_ref[...] = o.astype(o_ref.dtype)


def _ctx_call(q, q_col, k, k_col, v, v_col, *, batch, ctx, n_heads, n_stack, diff=False,
              scale=1.0, sink=None, lam_rows=None, g_sub=None, lam_init=0.0):
    qw = LANE if diff else n_stack * LANE
    qb = q_col * LANE // qw
    if lam_rows is None:
        lam_rows = jnp.zeros((8, LANE), F32)
        g_sub = jnp.ones((HEAD_DIM,), F32)
    use_sink = sink is not None
    if sink is None:
        sink = jnp.zeros((N_GROUP_HEADS,), F32)
    kern = functools.partial(_ctx_kernel, diff=diff, n_stack=n_stack, scale=float(scale),
                             use_sink=use_sink, lam_init=float(lam_init))
    return pl.pallas_call(
        kern,
        out_shape=jax.ShapeDtypeStruct((batch * ctx, GROUP_W), BF16),
        grid=(batch, n_heads),
        in_specs=[pl.BlockSpec(memory_space=pltpu.SMEM),
                  pl.BlockSpec((ctx, qw), lambda b, h: (b, qb + h)),
                  pl.BlockSpec((ctx, LANE), lambda b, h: (b, k_col + h)),
                  pl.BlockSpec((ctx, LANE), lambda b, h: (b, v_col + h)),
                  pl.BlockSpec((8, LANE), lambda b, h: (0, 0)),
                  pl.BlockSpec((1, HEAD_DIM), lambda b, h: (0, 0))],
        out_specs=pl.BlockSpec((ctx, qw), lambda b, h: (b, h)),
        compiler_params=_cparams(("parallel", "parallel")),
        name="attn_ctx",
    )(sink, q, k, v, lam_rows, g_sub.reshape(1, HEAD_DIM))


def _ffn_up_kernel(a_ref, ap_ref, an_ref, wg_ref, wv_ref, cwg_ref, cwv_ref, o_ref, ext_ref, *,
                   tiles_per_seq):
    i = pl.program_id(0)
    tm = a_ref.shape[0]

    @pl.when(pl.program_id(1) == 0)
    def _():
        first = (i % tiles_per_seq) == 0
        last = (i % tiles_per_seq) == tiles_per_seq - 1
        zero = jnp.zeros(ap_ref.shape, ap_ref.dtype)
        ext_ref[0:HALO, :] = jnp.where(first, zero, ap_ref[...])
        ext_ref[HALO:HALO + tm, :] = a_ref[...]
        ext_ref[HALO + tm:, :] = jnp.where(last, zero, an_ref[...])

    a = ext_ref[...]
    rows = min(EPI_ROWS, tm)

    def conv(u, r0, cw_ref):
        w = u[HALO + r0 - 8:HALO + r0 + rows + 8]
        n = w.shape[0]
        cu = (pltpu.roll(w, 1, 0) * cw_ref[0:1, :] + w * cw_ref[1:2, :]
              + pltpu.roll(w, n - 1, 0) * cw_ref[2:3, :])
        return cu[8:8 + rows]

    ug = jnp.dot(a, wg_ref[...].astype(BF16), preferred_element_type=F32)
    uv = jnp.dot(a, wv_ref[...].astype(BF16), preferred_element_type=F32)
    for r0 in range(0, tm, rows):
        gate = conv(ug, r0, cwg_ref)
        val = conv(uv, r0, cwv_ref)
        o_ref[r0:r0 + rows, :] = (gate * jax.nn.sigmoid(gate) * val).astype(o_ref.dtype)


def _ffn_up_call(a, w_up, layer, conv_w, seq):
    m, kdim = a.shape
    f = w_up.shape[2] // 2
    tm = min(1024, seq)
    tn = _pick(f, (256, 128))
    tiles_per_seq = seq // tm
    fb = f // tn
    hb = tm // HALO
    n_halo = m // HALO
    cw = jnp.pad(conv_w, ((0, 8 - CONV_W), (0, 0)))
    return pl.pallas_call(
        functools.partial(_ffn_up_kernel, tiles_per_seq=tiles_per_seq),
        out_shape=jax.ShapeDtypeStruct((m, f), BF16),
        grid=(m // tm, fb),
        in_specs=[pl.BlockSpec((tm, kdim), lambda i, j: (i, 0)),
                  pl.BlockSpec((HALO, kdim), lambda i, j: (jnp.maximum(i * hb - 1, 0), 0)),
                  pl.BlockSpec((HALO, kdim),
                               lambda i, j: (jnp.minimum((i + 1) * hb, n_halo - 1), 0)),
                  pl.BlockSpec((None, kdim, tn), lambda i, j: (layer, 0, j)),
                  pl.BlockSpec((None, kdim, tn), lambda i, j: (layer, 0, fb + j)),
                  pl.BlockSpec((8, tn), lambda i, j: (0, j)),
                  pl.BlockSpec((8, tn), lambda i, j: (0, fb + j))],
        out_specs=pl.BlockSpec((tm, tn), lambda i, j: (i, j)),
        scratch_shapes=[pltpu.VMEM((tm + 2 * HALO, kdim), BF16)],
        compiler_params=_cparams(("parallel", "arbitrary")),
        name="ffn_up",
    )(a, a, a, w_up, w_up, cw, cw)


def _conv_ffn(hx2d, w_up, layer, conv_w, w_down, seq):
    g = _ffn_up_call(hx2d, w_up, layer, conv_w, seq)
    return _matmul_call(g, w_down, BF16)


def kernel(x, c, ctx, c_ctx, w_ada, b_ada, g_attn_pre, g_attn_post, g_mlp_pre, g_mlp_post,
           w_in, w_out, g_q_a, g_k_a, lam_q1, lam_k1, lam_q2, lam_k2, g_sub_b, sink_c, na_bias,
           w_up, conv_ffn_w, w_down):
    batch, seq, d = x.shape
    n_ctx = ctx.shape[1]
    depth = w_ada.shape[0]
    assert batch + 1 <= 8 and d == N_Q

    cond = jnp.zeros((8, d), F32).at[:batch].set(c).at[batch].set(c_ctx)
    mod_all = _ada_call(cond, w_ada, b_ada)
    rope_hd = _rope_tables(seq, HEAD_DIM)
    rope_dh = _rope_tables(seq, DH_B)
    scale_hd = HEAD_DIM ** -0.5
    scale_dh = DH_B ** -0.5

    def latent_mod(l):
        mod = mod_all[l, :batch].reshape(batch, 6, d)
        return tuple(mod[:, k] for k in range(6))

    h = ctx
    sh_a, sc_a, gt_a, sh_m, sc_m, gt_m = latent_mod(0)
    hx = _norm_mod_call(x, g_attn_pre[0], sc_a, sh_a)
    for l in range(depth):
        last = l == depth - 1
        lam_init = 0.8 - 0.6 * math.exp(-0.3 * l)
        sh_a, sc_a, gt_a, sh_m, sc_m, gt_m = latent_mod(l)
        mod_c = mod_all[l, batch].reshape(6, d)
        one = lambda v: v.reshape(1, d)
        wd = w_down[l].astype(BF16)
        lam_rows = jnp.zeros((8, LANE), F32).at[:4, :DH_B].set(
            jnp.stack([lam_q1[l], lam_k1[l], lam_q2[l], lam_k2[l]]))

        hx = hx.reshape(batch * seq, d)
        hc = _norm_mod_call(h.reshape(1, batch * n_ctx, d), g_attn_pre[l],
                            one(mod_c[1]), one(mod_c[0])).reshape(batch * n_ctx, d)
        p = _matmul_ws_call(hx, w_in, l, BF16)
        pc_off = 32 if last else 0
        pc = _matmul_ws_call(hc, w_in, l, BF16, col0=pc_off * LANE)

        dims = dict(batch=batch, seq=seq, ctx=n_ctx)
        q_c = _prep_call(p, QC_COL, GROUP_W, seq, rope=rope_hd, scale=scale_hd)
        k_c = _prep_call(p, KC_COL, KV_GQA * LANE, seq, rope=rope_hd)
        q_a = _prep_call(p, QA_COL, GROUP_W, seq, gain=g_q_a[l], rope=rope_hd,
                         scale=scale_hd * LOG2E)
        o_a = _flash_call(p, pc, pc_off, QA_COL, KA_COL, VA_COL, rope_hd, 1.0, diff=False,
                          q_ready=q_a, g_k=g_k_a[l], **dims)
        o_b = _flash_call(p, pc, pc_off, QB_COL, KB_COL, VB_COL, rope_dh, scale_dh * LOG2E,
                          diff=True, lam_rows=lam_rows, g_sub=g_sub_b[l], lam_init=lam_init, **dims)
        o_c = _band_call(sink_c[l], q_c, k_c, p, pc, pc_off, **dims)
        o_d = _nbr_call(p, pc, pc_off, na_bias[l], **dims)
        o = jnp.concatenate([o_a, o_b, o_c, o_d], axis=-1)
        y = _matmul_ws_call(o, w_out, l, BF16)
        x, hx = _norm_res_mod_call(x, y, g_attn_post[l], gt_a, g_mlp_pre[l], sc_m, sh_m)

        y = _conv_ffn(hx.reshape(batch * seq, d), w_up, l, conv_ffn_w[l], wd, seq)
        if last:
            x = _norm_res_call(x, y, g_mlp_post[l], gt_m)
        else:
            nxt = latent_mod(l + 1)
            x, hx = _norm_res_mod_call(x, y, g_mlp_post[l], gt_m, g_attn_pre[l + 1], nxt[1], nxt[0])

        if not last:
            cdims = dict(batch=batch, ctx=n_ctx)
            qc_a = _prep_call(pc, QA_COL, GROUP_W, n_ctx, gain=g_q_a[l], scale=scale_hd)
            kc_a = _prep_call(pc, KA_COL, KV_GQA * LANE, n_ctx, gain=g_k_a[l])
            co_a = _ctx_call(qc_a, 0, kc_a, 0, pc, VA_COL, n_heads=KV_GQA, n_stack=GQA_G, **cdims)
            co_b = _ctx_call(pc, QB_COL, pc, KB_COL, pc, VB_COL, n_heads=N_GROUP_HEADS,
                             n_stack=2, diff=True, scale=scale_dh, lam_rows=lam_rows,
                             g_sub=g_sub_b[l], lam_init=lam_init, **cdims)
            co_c = _ctx_call(pc, QC_COL, pc, KC_COL, pc, VC_COL, n_heads=KV_GQA, n_stack=GQA_G,
                             scale=scale_hd, sink=sink_c[l], **cdims)
            co_d = _ctx_call(pc, QD_COL, pc, KD_COL, pc, VD_COL, n_heads=N_GROUP_HEADS,
                             n_stack=1, scale=scale_hd, **cdims)
            co = jnp.concatenate([co_a, co_b, co_c, co_d], axis=-1)
            yc = _matmul_ws_call(co, w_out, l, BF16)
            h3 = h.reshape(1, batch * n_ctx, d)
            h3 = _norm_res_call(h3, yc, g_attn_post[l], one(mod_c[2]))
            hc = _norm_mod_call(h3, g_mlp_pre[l], one(mod_c[4]), one(mod_c[3]))
            yc = _conv_ffn(hc.reshape(batch * n_ctx, d), w_up, l, conv_ffn_w[l], wd, n_ctx)
            h = _norm_res_call(h3, yc, g_mlp_post[l], one(mod_c[5])).reshape(batch, n_ctx, d)
    return x
```

```python
import functools
import math

import jax
import jax.numpy as jnp
import numpy as np
from jax import lax
from jax.experimental import pallas as pl
from jax.experimental.pallas import tpu as pltpu

F32 = jnp.float32
BF16 = jnp.bfloat16

GRID_W = 64
HEAD_DIM = 128
N_GROUP_HEADS = 8
KV_GQA = 2
GQA_G = N_GROUP_HEADS // KV_GQA
DH_B = HEAD_DIM // 2
WIN_C = 128
NA_KH = 8
NA_KW = 16
NBR_GROUP = 8
NBR_SLAB = 16
CONV_W = 3
ROPE_THETA = 10000.0
EPS = 1e-6
NEG_INF = -1e30
LOG2E = math.log2(math.e)
GROUP_W = N_GROUP_HEADS * HEAD_DIM
N_Q = 4 * GROUP_W

QA_COL, QB_COL, QC_COL, QD_COL = 0, 8, 16, 24
KA_COL, VA_COL = 32, 34
KB_COL, VB_COL = 36, 44
KC_COL, VC_COL = 52, 54
KD_COL, VD_COL = 56, 64
KV_COLS = 40

LANE = 128
HALO = 16
EPI_ROWS = 128
VMEM_LIMIT = 56 * 1024 * 1024


def _cparams(sem):
    return pltpu.CompilerParams(dimension_semantics=sem, vmem_limit_bytes=VMEM_LIMIT)


def _ada_kernel(a_ref, w_ref, b_ref, o_ref):
    a = a_ref[...]
    a = a * jax.nn.sigmoid(a)
    o_ref[...] = jnp.dot(a.astype(BF16), w_ref[...].astype(BF16),
                         preferred_element_type=F32) + b_ref[...]


def _ada_call(cond, w_ada, b_ada):
    n_layers, d, n = w_ada.shape
    tn = 512
    return pl.pallas_call(
        _ada_kernel,
        out_shape=jax.ShapeDtypeStruct((n_layers, 8, n), F32),
        grid=(n_layers, n // tn),
        in_specs=[pl.BlockSpec((8, d), lambda l, j: (0, 0)),
                  pl.BlockSpec((None, d, tn), lambda l, j: (l, 0, j)),
                  pl.BlockSpec((None, 1, tn), lambda l, j: (l, 0, j))],
        out_specs=pl.BlockSpec((None, 8, tn), lambda l, j: (l, 0, j)),
        compiler_params=_cparams(("parallel", "parallel")),
        name="ada_mod",
    )(cond, w_ada, b_ada.reshape(n_layers, 1, n))


def _rms(x, g):
    return x * lax.rsqrt(jnp.mean(x * x, axis=-1, keepdims=True) + EPS) * g


def _norm_mod_kernel(x_ref, g_ref, sc_ref, sh_ref, o_ref):
    y = _rms(x_ref[...], g_ref[...])
    o_ref[...] = (y * (1.0 + sc_ref[...]) + sh_ref[...]).astype(o_ref.dtype)


def _norm_mod_call(x, g, sc, sh):
    nb, r, d = x.shape
    tr = min(256, r)
    return pl.pallas_call(
        _norm_mod_kernel,
        out_shape=jax.ShapeDtypeStruct((nb, r, d), BF16),
        grid=(nb, r // tr),
        in_specs=[pl.BlockSpec((None, tr, d), lambda b, i: (b, i, 0)),
                  pl.BlockSpec((1, d), lambda b, i: (0, 0)),
                  pl.BlockSpec((None, 1, d), lambda b, i: (b, 0, 0)),
                  pl.BlockSpec((None, 1, d), lambda b, i: (b, 0, 0))],
        out_specs=pl.BlockSpec((None, tr, d), lambda b, i: (b, i, 0)),
        compiler_params=_cparams(("parallel", "parallel")),
        name="norm_mod",
    )(x, g.reshape(1, d), sc.reshape(nb, 1, d), sh.reshape(nb, 1, d))


def _norm_res_kernel(x_ref, y_ref, g_ref, gt_ref, o_ref):
    o_ref[...] = x_ref[...] + gt_ref[...] * _rms(y_ref[...].astype(F32), g_ref[...])


def _norm_res_call(x, y, g, gate):
    nb, r, d = x.shape
    tr = min(256, r)
    return pl.pallas_call(
        _norm_res_kernel,
        out_shape=jax.ShapeDtypeStruct((nb, r, d), F32),
        grid=(nb, r // tr),
        in_specs=[pl.BlockSpec((None, tr, d), lambda b, i: (b, i, 0)),
                  pl.BlockSpec((None, tr, d), lambda b, i: (b, i, 0)),
                  pl.BlockSpec((1, d), lambda b, i: (0, 0)),
                  pl.BlockSpec((None, 1, d), lambda b, i: (b, 0, 0))],
        out_specs=pl.BlockSpec((None, tr, d), lambda b, i: (b, i, 0)),
        compiler_params=_cparams(("parallel", "parallel")),
        name="norm_res",
    )(x, y.reshape(nb, r, d), g.reshape(1, d), gate.reshape(nb, 1, d))


def _norm_res_mod_kernel(x_ref, y_ref, g_ref, gt_ref, g2_ref, sc_ref, sh_ref, o_ref, h_ref):
    x = x_ref[...] + gt_ref[...] * _rms(y_ref[...].astype(F32), g_ref[...])
    o_ref[...] = x
    h_ref[...] = (_rms(x, g2_ref[...]) * (1.0 + sc_ref[...]) + sh_ref[...]).astype(h_ref.dtype)


def _norm_res_mod_call(x, y, g, gate, g_next, sc, sh):
    nb, r, d = x.shape
    tr = min(256, r)
    row = pl.BlockSpec((None, tr, d), lambda b, i: (b, i, 0))
    vec = pl.BlockSpec((None, 1, d), lambda b, i: (b, 0, 0))
    gain = pl.BlockSpec((1, d), lambda b, i: (0, 0))
    return pl.pallas_call(
        _norm_res_mod_kernel,
        out_shape=(jax.ShapeDtypeStruct((nb, r, d), F32), jax.ShapeDtypeStruct((nb, r, d), BF16)),
        grid=(nb, r // tr),
        in_specs=[row, row, gain, vec, gain, vec, vec],
        out_specs=(row, row),
        compiler_params=_cparams(("parallel", "parallel")),
        name="norm_res_mod",
    )(x, y.reshape(nb, r, d), g.reshape(1, d), gate.reshape(nb, 1, d), g_next.reshape(1, d),
      sc.reshape(nb, 1, d), sh.reshape(nb, 1, d))


def _matmul_kernel(a_ref, w_ref, o_ref, acc_ref, *, nk):
    k = pl.program_id(2)
    part = jnp.dot(a_ref[...], w_ref[...], preferred_element_type=F32)
    if nk == 1:
        o_ref[...] = part.astype(o_ref.dtype)
    else:
        @pl.when(k == 0)
        def _():
            acc_ref[...] = part

        @pl.when(jnp.logical_and(k > 0, k < nk - 1))
        def _():
            acc_ref[...] += part

        @pl.when(k == nk - 1)
        def _():
            o_ref[...] = (acc_ref[...] + part).astype(o_ref.dtype)


def _pick(n, prefs):
    for t in prefs:
        if n % t == 0:
            return t
    return n


def _matmul_call(a, w, out_dtype):
    m, kdim = a.shape
    _, n = w.shape
    tm = _pick(m, (1024, 512, 256))
    tn = _pick(n, (512, 256, 128))
    tk = kdim if kdim <= 4096 else _pick(kdim, (5504, 4096, 2048))
    nk = kdim // tk
    return pl.pallas_call(
        functools.partial(_matmul_kernel, nk=nk),
        out_shape=jax.ShapeDtypeStruct((m, n), out_dtype),
        grid=(m // tm, n // tn, nk),
        in_specs=[pl.BlockSpec((tm, tk), lambda i, j, k: (i, k)),
                  pl.BlockSpec((tk, tn), lambda i, j, k: (k, j))],
        out_specs=pl.BlockSpec((tm, tn), lambda i, j, k: (i, j)),
        scratch_shapes=[pltpu.VMEM((tm, tn) if nk > 1 else (8, LANE), F32)],
        compiler_params=_cparams(("parallel", "parallel", "arbitrary")),
        name="matmul",
    )(a, w)


def _matmul_ws_kernel(a_ref, w_ref, o_ref, wb_ref):
    @pl.when(pl.program_id(1) == 0)
    def _():
        wb_ref[...] = w_ref[...].astype(BF16)

    o_ref[...] = jnp.dot(a_ref[...], wb_ref[...], preferred_element_type=F32).astype(o_ref.dtype)


def _matmul_ws_call(a, w, layer, out_dtype, col0=0):
    m, kdim = a.shape
    n = w.shape[2] - col0
    tm = _pick(m, (1024, 512, 256))
    tn = _pick(n, (512, 256, 128))
    assert col0 % tn == 0
    cb0 = col0 // tn
    return pl.pallas_call(
        _matmul_ws_kernel,
        out_shape=jax.ShapeDtypeStruct((m, n), out_dtype),
        grid=(n // tn, m // tm),
        in_specs=[pl.BlockSpec((tm, kdim), lambda j, i: (i, 0)),
                  pl.BlockSpec((None, kdim, tn), lambda j, i: (layer, 0, cb0 + j))],
        out_specs=pl.BlockSpec((tm, tn), lambda j, i: (i, j)),
        scratch_shapes=[pltpu.VMEM((kdim, tn), BF16)],
        compiler_params=_cparams(("parallel", "arbitrary")),
        name="matmul_ws",
    )(a, w)


def _swap_pairs(y):
    lane = lax.broadcasted_iota(jnp.int32, y.shape, 1)
    nxt = pltpu.roll(y, LANE - 1, 1)
    prv = pltpu.roll(y, 1, 1)
    return jnp.where((lane & 1) == 0, nxt, prv)


def _prep_tile(y, gain, cos, sin, scale):
    y = y.astype(F32)
    if gain is not None:
        y = _rms(y, gain)
    if cos is not None:
        y = y * cos + _swap_pairs(y) * sin
    if scale != 1.0:
        y = y * scale
    return y.astype(BF16)


def _prep_kernel(p_ref, cos_ref, sin_ref, g_ref, o_ref, *, n_chunks, norm, rope, scale):
    for c in range(n_chunks):
        sl = slice(c * LANE, (c + 1) * LANE)
        o_ref[:, sl] = _prep_tile(p_ref[:, sl], g_ref[...] if norm else None,
                                  cos_ref[...] if rope else None,
                                  sin_ref[...] if rope else None, scale)


def _prep_call(p, col0, width, seq, *, gain=None, rope=None, scale=1.0):
    rows = p.shape[0]
    tr = min(256, seq)
    n_seq_tiles = seq // tr
    bw = min(width, 512)
    assert (col0 * LANE) % bw == 0 and width % bw == 0
    cb0 = col0 * LANE // bw
    if rope is None:
        cos = sin = jnp.zeros((tr, LANE), F32)
        tab_map = lambda i, j: (0, 0)
    else:
        cos, sin = rope
        tab_map = lambda i, j: (i % n_seq_tiles, 0)
    g = (jnp.ones((HEAD_DIM,), F32) if gain is None else gain).reshape(1, HEAD_DIM)
    kern = functools.partial(_prep_kernel, n_chunks=bw // LANE, norm=gain is not None,
                             rope=rope is not None, scale=float(scale))
    return pl.pallas_call(
        kern,
        out_shape=jax.ShapeDtypeStruct((rows, width), BF16),
        grid=(rows // tr, width // bw),
        in_specs=[pl.BlockSpec((tr, bw), lambda i, j: (i, cb0 + j)),
                  pl.BlockSpec((tr, LANE), tab_map),
                  pl.BlockSpec((tr, LANE), tab_map),
                  pl.BlockSpec((1, HEAD_DIM), lambda i, j: (0, 0))],
        out_specs=pl.BlockSpec((tr, bw), lambda i, j: (i, j)),
        compiler_params=_cparams(("parallel", "parallel")),
        name="qk_prep",
    )(p, cos, sin, g)


def _rope_tables(seq, head_dim):
    t = np.arange(seq)
    rows = (t // GRID_W).astype(np.float32)
    cols = (t % GRID_W).astype(np.float32)
    quarter = head_dim // 4
    inv = (ROPE_THETA ** (-np.arange(quarter, dtype=np.float32) / quarter)).astype(np.float32)
    ang = np.concatenate([rows[:, None] * inv, cols[:, None] * inv], axis=-1)
    lane = np.arange(LANE)
    pair = (lane % head_dim) // 2
    sign = np.where(lane % 2 == 0, -1.0, 1.0).astype(np.float32)
    cos = np.cos(ang).astype(np.float32)[:, pair]
    sin = np.sin(ang).astype(np.float32)[:, pair] * sign
    return jnp.asarray(cos), jnp.asarray(sin)


def _qk(q, k):
    return lax.dot_general(q, k, (((1,), (1,)), ((), ())), preferred_element_type=F32)


def _stack_heads(q, n):
    if n == 1:
        return q
    return jnp.concatenate([q[:, g * LANE:(g + 1) * LANE] for g in range(n)], axis=0)


def _unstack_heads(o, n):
    if n == 1:
        return o
    t = o.shape[0] // n
    return jnp.concatenate([o[g * t:(g + 1) * t] for g in range(n)], axis=1)


def _stack_diff(q):
    lane = lax.broadcasted_iota(jnp.int32, q.shape, 1)
    zero = jnp.zeros_like(q)
    return jnp.concatenate([jnp.where(lane < DH_B, q, zero),
                            jnp.where(lane >= DH_B, q, zero)], axis=0)


def _softmax_attend(groups, sink=None):
    m = None
    for s, _ in groups:
        sm = jnp.max(s, axis=-1, keepdims=True)
        m = sm if m is None else jnp.maximum(m, sm)
    if sink is not None:
        m = jnp.maximum(m, sink)
    l = None
    acc = None
    for s, v in groups:
        p = jnp.exp(s - m)
        ps = jnp.sum(p, axis=-1, keepdims=True)
        pv = jnp.dot(p.astype(v.dtype), v, preferred_element_type=F32)
        l = ps if l is None else l + ps
        acc = pv if acc is None else acc + pv
    if sink is not None:
        l = l + jnp.exp(sink - m)
    return acc / l


def _diff_lambda(lam_ref, lam_init):
    a = jnp.sum(lam_ref[0:1, :] * lam_ref[1:2, :], axis=-1, keepdims=True)
    b = jnp.sum(lam_ref[2:3, :] * lam_ref[3:4, :], axis=-1, keepdims=True)
    return jnp.exp(a) - jnp.exp(b) + lam_init


def _diff_finalize(o, lam_ref, gsub_ref, lam_init):
    t = o.shape[0] // 2
    d = o[:t] - _diff_lambda(lam_ref, lam_init) * o[t:]
    return _rms(d, gsub_ref[...]) * (1.0 - lam_init)


def _sink_rows(sink_ref, kv, rows_per_head, n_rows):
    head = lax.broadcasted_iota(jnp.int32, (n_rows, 1), 0) // rows_per_head
    out = jnp.zeros((n_rows, 1), F32)
    for g in range(GQA_G):
        out = jnp.where(head == g, sink_ref[kv * GQA_G + g], out)
    return out


def _lane_tiles(s):
    return [s[:, j * LANE:(j + 1) * LANE] for j in range(s.shape[1] // LANE)]


def _flash_kernel(q_ref, kl_ref, vl_ref, kc_ref, vc_ref, cq_ref, sq_ref, ck_ref, sk_ref,
                  gq_ref, gk_ref, lam_ref, gsub_ref, o_ref, kp_ref, kcp_ref, *,
                  diff, n_stack, tk, lam_init, norm, q_scale, q_ready):
    gq = gq_ref[...] if norm else None
    gk = gk_ref[...] if norm else None

    @pl.when(pl.program_id(2) == 0)
    def _():
        for c in range(kl_ref.shape[0] // tk):
            sl = slice(c * tk, (c + 1) * tk)
            kp_ref[sl, :] = _prep_tile(kl_ref[sl, :], gk, ck_ref[sl, :], sk_ref[sl, :], 1.0)
        kcp_ref[...] = _prep_tile(kc_ref[...], gk, None, None, 1.0)

    kl_ref, kc_ref = kp_ref, kcp_ref
    if q_ready:
        q = q_ref[...]
    else:
        cq, sq = cq_ref[...], sq_ref[...]
        q = jnp.concatenate([_prep_tile(q_ref[:, g * LANE:(g + 1) * LANE], gq, cq, sq, q_scale)
                             for g in range(q_ref.shape[1] // LANE)], axis=1)
    q = _stack_diff(q) if diff else _stack_heads(q, n_stack)

    def row_max(tiles):
        m = tiles[0]
        for t in tiles[1:]:
            m = jnp.maximum(m, t)
        return jnp.broadcast_to(jnp.max(m, axis=-1, keepdims=True), m.shape)

    def weights(tiles, m, v):
        ps = [jnp.exp2(t - m) for t in tiles]
        l = ps[0]
        for p in ps[1:]:
            l = l + p
        pv = jnp.dot(jnp.concatenate(ps, axis=1).astype(v.dtype), v, preferred_element_type=F32)
        return l, pv

    def latent_scores(c):
        return _lane_tiles(_qk(q, kl_ref[c * tk:(c + 1) * tk, :]))

    n_chunks = kl_ref.shape[0] // tk
    tiles = _lane_tiles(_qk(q, kc_ref[...]))
    nxt = latent_scores(0)
    m = row_max(tiles)
    l, acc = weights(tiles, m, vc_ref[...])
    for c in range(n_chunks):
        tiles = nxt
        if c + 1 < n_chunks:
            nxt = latent_scores(c + 1)
        m_new = jnp.maximum(m, row_max(tiles))
        alpha = jnp.exp2(m - m_new)
        l_c, pv = weights(tiles, m_new, vl_ref[c * tk:(c + 1) * tk, :])
        l = alpha * l + l_c
        acc = alpha * acc + pv
        m = m_new
    o = acc / jnp.sum(l, axis=-1, keepdims=True)
    if diff:
        o = _diff_finalize(o, lam_ref, gsub_ref, lam_init)
    else:
        o = _unstack_heads(o, n_stack)
    o_ref[...] = o.astype(o_ref.dtype)


def _flash_call(p, pc, pc_off, q_col, k_col, v_col, rope, q_scale, *, batch, seq, ctx, diff,
                q_ready=None, g_q=None, g_k=None, lam_rows=None, g_sub=None, lam_init=0.0):
    n_heads = N_GROUP_HEADS if diff else KV_GQA
    n_stack = 2 if diff else GQA_G
    tq = min(1024 // n_stack, seq)
    qw = LANE if diff else GQA_G * LANE
    qb = 0 if q_ready is not None else q_col * LANE // qw
    nq = seq // tq
    norm = g_k is not None
    ones = jnp.ones((HEAD_DIM,), F32)
    if lam_rows is None:
        lam_rows = jnp.zeros((8, LANE), F32)
    vec = lambda g: (ones if g is None else g).reshape(1, HEAD_DIM)
    cos, sin = rope
    kern = functools.partial(_flash_kernel, diff=diff, n_stack=n_stack, tk=min(1024, seq),
                             lam_init=float(lam_init), norm=norm, q_scale=float(q_scale),
                             q_ready=q_ready is not None)
    const = lambda b, h, i: (0, 0)
    return pl.pallas_call(
        kern,
        out_shape=jax.ShapeDtypeStruct((batch * seq, GROUP_W), BF16),
        grid=(batch, n_heads, nq),
        in_specs=[pl.BlockSpec((tq, qw), lambda b, h, i: (b * nq + i, qb + h)),
                  pl.BlockSpec((seq, LANE), lambda b, h, i: (b, k_col + h)),
                  pl.BlockSpec((seq, LANE), lambda b, h, i: (b, v_col + h)),
                  pl.BlockSpec((ctx, LANE), lambda b, h, i: (b, k_col - pc_off + h)),
                  pl.BlockSpec((ctx, LANE), lambda b, h, i: (b, v_col - pc_off + h)),
                  pl.BlockSpec((tq, LANE), lambda b, h, i: (i, 0)),
                  pl.BlockSpec((tq, LANE), lambda b, h, i: (i, 0)),
                  pl.BlockSpec((seq, LANE), const),
                  pl.BlockSpec((seq, LANE), const),
                  pl.BlockSpec((1, HEAD_DIM), const),
                  pl.BlockSpec((1, HEAD_DIM), const),
                  pl.BlockSpec((8, LANE), const),
                  pl.BlockSpec((1, HEAD_DIM), const)],
        out_specs=pl.BlockSpec((tq, qw), lambda b, h, i: (b * nq + i, h)),
        scratch_shapes=[pltpu.VMEM((seq, LANE), BF16), pltpu.VMEM((ctx, LANE), BF16)],
        compiler_params=_cparams(("parallel", "parallel", "arbitrary")),
        name="attn_full",
    )(p if q_ready is None else q_ready, p, p, pc, pc, cos, sin, cos, sin, vec(g_q), vec(g_k),
      lam_rows, vec(g_sub))


def _band_kernel(sink_ref, q_ref, kp_ref, kx_ref, kn_ref, vp_ref, vx_ref, vn_ref,
                 kc_ref, vc_ref, o_ref, *, tq, seq):
    kv = pl.program_id(1)
    i = pl.program_id(2)
    q = _stack_heads(q_ref[...], GQA_G)
    rows = q.shape[0]
    k_loc = jnp.concatenate([kp_ref[...], kx_ref[...], kn_ref[...]], axis=0)
    v_loc = jnp.concatenate([vp_ref[...], vx_ref[...], vn_ref[...]], axis=0)
    s_loc = _qk(q, k_loc)
    qpos = i * tq + (lax.broadcasted_iota(jnp.int32, s_loc.shape, 0) & (tq - 1))
    kpos = (i - 1) * tq + lax.broadcasted_iota(jnp.int32, s_loc.shape, 1)
    ok = jnp.where(kpos >= 0, jnp.abs(qpos - kpos), WIN_C + 1)
    ok = jnp.where(kpos < seq, ok, WIN_C + 1) <= WIN_C
    s_loc = jnp.where(ok, s_loc, NEG_INF)
    s_ctx = _qk(q, kc_ref[...])
    sink = _sink_rows(sink_ref, kv, tq, rows)
    o = _softmax_attend([(s_loc, v_loc), (s_ctx, vc_ref[...])], sink)
    o_ref[...] = _unstack_heads(o, GQA_G).astype(o_ref.dtype)


def _band_call(sink, q, kl, p, pc, pc_off, *, batch, seq, ctx):
    tq = WIN_C
    nq = seq // tq
    qw = GQA_G * LANE

    def nb(delta, col):
        def imap(b, h, i):
            return (b * nq + jnp.clip(i + delta, 0, nq - 1), col + h)
        return imap

    return pl.pallas_call(
        functools.partial(_band_kernel, tq=tq, seq=seq),
        out_shape=jax.ShapeDtypeStruct((batch * seq, GROUP_W), BF16),
        grid=(batch, KV_GQA, nq),
        in_specs=[pl.BlockSpec(memory_space=pltpu.SMEM),
                  pl.BlockSpec((tq, qw), lambda b, h, i: (b * nq + i, h)),
                  pl.BlockSpec((tq, LANE), nb(-1, 0)),
                  pl.BlockSpec((tq, LANE), nb(0, 0)),
                  pl.BlockSpec((tq, LANE), nb(1, 0)),
                  pl.BlockSpec((tq, LANE), nb(-1, VC_COL)),
                  pl.BlockSpec((tq, LANE), nb(0, VC_COL)),
                  pl.BlockSpec((tq, LANE), nb(1, VC_COL)),
                  pl.BlockSpec((ctx, LANE), lambda b, h, i: (b, KC_COL - pc_off + h)),
                  pl.BlockSpec((ctx, LANE), lambda b, h, i: (b, VC_COL - pc_off + h))],
        out_specs=pl.BlockSpec((tq, qw), lambda b, h, i: (b * nq + i, h)),
        compiler_params=_cparams(("parallel", "parallel", "parallel")),
        name="attn_band",
    )(sink, q, kl, kl, kl, p, p, p, pc, pc)


def _nbr_build_bias(tab_ref, bias_ref):
    shape = (GRID_W, LANE)
    w = lax.broadcasted_iota(jnp.int32, shape, 0)
    lane = lax.broadcasted_iota(jnp.int32, shape, 1)
    c = lane & (GRID_W - 1)
    cs = jnp.clip(w - NA_KW // 2, 0, GRID_W - NA_KW)
    d = c - cs
    mask = jnp.where(d >= 0, jnp.where(d < NA_KW, 0.0, NEG_INF), NEG_INF).astype(F32)
    neg = jnp.full(shape, NEG_INF, F32)
    toep = [pltpu.roll(jnp.broadcast_to(tab_ref[ro:ro + 1, :], shape), LANE - (NA_KW - 1), 1,
                       stride=1, stride_axis=0) + mask for ro in range(2 * NA_KH - 1)]
    kinds = (lambda dr: (dr, max(dr - NA_KH // 2, 0)),
             lambda dr: (NA_KH // 2 + dr, dr),
             lambda dr: (NBR_GROUP + dr, min(NA_KH // 2 + dr, NBR_SLAB - NA_KH)))
    for kind, rel in enumerate(kinds):
        for dr in range(NBR_GROUP):
            r_rel, rs_rel = rel(dr)
            for j in range(NBR_SLAB // 2):
                halves = [toep[kr - r_rel + NA_KH - 1] if rs_rel <= kr < rs_rel + NA_KH else neg
                          for kr in (2 * j, 2 * j + 1)]
                tile = neg if halves[0] is neg and halves[1] is neg else jnp.where(
                    lane < GRID_W, halves[0], halves[1])
                bias_ref[kind, dr * GRID_W:(dr + 1) * GRID_W, j * LANE:(j + 1) * LANE] = tile


def _nbr_kernel(q_ref, k_ref, v_ref, kc_ref, vc_ref, tab_ref, o_ref, bias_ref, *, n_rows, scale):
    n_groups = n_rows // NBR_GROUP
    gq = NBR_GROUP * GRID_W
    slab = NBR_SLAB * GRID_W
    kc = kc_ref[...]
    vc = vc_ref[...]
    _nbr_build_bias(tab_ref, bias_ref)

    def keys_of(g):
        base = min(max(g * NBR_GROUP - NA_KH // 2, 0), n_rows - NBR_SLAB)
        return slice(base * GRID_W, base * GRID_W + slab)

    def scores(g):
        q = q_ref[g * gq:(g + 1) * gq, :]
        return _qk(q, k_ref[keys_of(g), :]), _qk(q, kc)

    nxt = scores(0)
    for g in range(n_groups):
        raw_loc, raw_ctx = nxt
        if g + 1 < n_groups:
            nxt = scores(g + 1)
        kind = 0 if g == 0 else (2 if g == n_groups - 1 else 1)
        s_loc = raw_loc * scale + bias_ref[kind]
        o = _softmax_attend([(s_loc, v_ref[keys_of(g), :]), (raw_ctx * scale, vc)])
        o_ref[g * gq:(g + 1) * gq, :] = o.astype(o_ref.dtype)


def _nbr_table(na_bias):
    h, nr, nc = na_bias.shape
    t = jnp.pad(na_bias.astype(F32), ((0, 0), (0, 16 - nr), (0, GRID_W - nc)))
    return jnp.concatenate([t] * (LANE // GRID_W), axis=-1)


def _nbr_call(p, pc, pc_off, na_bias, *, batch, seq, ctx):
    n_rows = seq // GRID_W
    assert n_rows >= NBR_SLAB and n_rows % NBR_GROUP == 0 and 2 * GRID_W == LANE
    assert NBR_SLAB >= NBR_GROUP + NA_KH - 1 and NBR_SLAB == NBR_GROUP + NA_KH
    scale = HEAD_DIM ** -0.5
    return pl.pallas_call(
        functools.partial(_nbr_kernel, n_rows=n_rows, scale=scale),
        out_shape=jax.ShapeDtypeStruct((batch * seq, GROUP_W), BF16),
        grid=(batch, N_GROUP_HEADS),
        in_specs=[pl.BlockSpec((seq, LANE), lambda b, h: (b, QD_COL + h)),
                  pl.BlockSpec((seq, LANE), lambda b, h: (b, KD_COL + h)),
                  pl.BlockSpec((seq, LANE), lambda b, h: (b, VD_COL + h)),
                  pl.BlockSpec((ctx, LANE), lambda b, h: (b, KD_COL - pc_off + h)),
                  pl.BlockSpec((ctx, LANE), lambda b, h: (b, VD_COL - pc_off + h)),
                  pl.BlockSpec((None, 16, LANE), lambda b, h: (h, 0, 0))],
        out_specs=pl.BlockSpec((seq, LANE), lambda b, h: (b, h)),
        scratch_shapes=[pltpu.VMEM((3, NBR_GROUP * GRID_W, NBR_SLAB * GRID_W), F32)],
        compiler_params=_cparams(("parallel", "parallel")),
        name="attn_nbr",
    )(p, p, p, pc, pc, _nbr_table(na_bias))


def _ctx_kernel(sink_ref, q_ref, k_ref, v_ref, lam_ref, gsub_ref, o_ref, *,
                diff, n_stack, scale, use_sink, lam_init):
    q = q_ref[...]
    q = _stack_diff(q) if diff else _stack_heads(q, n_stack)
    s = _qk(q, k_ref[...])
    if scale != 1.0:
        s = s * scale
    sink = None
    if use_sink:
        sink = _sink_rows(sink_ref, pl.program_id(1), q_ref.shape[0], q.shape[0])
    o = _softmax_attend([(s, v_ref[...])], sink)
    if diff:
        o = _diff_finalize(o, lam_ref, gsub_ref, lam_init)
    else:
        o = _unstack_heads(o, n_stack)
    o_ref[...] = o.astype(o_ref.dtype)


def _ctx_call(q, q_col, k, k_col, v, v_col, *, batch, ctx, n_heads, n_stack, diff=False,
              scale=1.0, sink=None, lam_rows=None, g_sub=None, lam_init=0.0):
    qw = LANE if diff else n_stack * LANE
    qb = q_col * LANE // qw
    if lam_rows is None:
        lam_rows = jnp.zeros((8, LANE), F32)
        g_sub = jnp.ones((HEAD_DIM,), F32)
    use_sink = sink is not None
    if sink is None:
        sink = jnp.zeros((N_GROUP_HEADS,), F32)
    kern = functools.partial(_ctx_kernel, diff=diff, n_stack=n_stack, scale=float(scale),
                             use_sink=use_sink, lam_init=float(lam_init))
    return pl.pallas_call(
        kern,
        out_shape=jax.ShapeDtypeStruct((batch * ctx, GROUP_W), BF16),
        grid=(batch, n_heads),
        in_specs=[pl.BlockSpec(memory_space=pltpu.SMEM),
                  pl.BlockSpec((ctx, qw), lambda b, h: (b, qb + h)),
                  pl.BlockSpec((ctx, LANE), lambda b, h: (b, k_col + h)),
                  pl.BlockSpec((ctx, LANE), lambda b, h: (b, v_col + h)),
                  pl.BlockSpec((8, LANE), lambda b, h: (0, 0)),
                  pl.BlockSpec((1, HEAD_DIM), lambda b, h: (0, 0))],
        out_specs=pl.BlockSpec((ctx, qw), lambda b, h: (b, h)),
        compiler_params=_cparams(("parallel", "parallel")),
        name="attn_ctx",
    )(sink, q, k, v, lam_rows, g_sub.reshape(1, HEAD_DIM))


def _ffn_up_kernel(a_ref, ap_ref, an_ref, wg_ref, wv_ref, cwg_ref, cwv_ref, *rest,
                   tiles_per_seq, cast_down):
    if cast_down:
        wd_ref, o_ref, wdb_ref, ext_ref = rest
        wdb_ref[...] = wd_ref[...].astype(BF16)
    else:
        o_ref, ext_ref = rest
    i = pl.program_id(0)
    tm = a_ref.shape[0]

    @pl.when(pl.program_id(1) == 0)
    def _():
        first = (i % tiles_per_seq) == 0
        last = (i % tiles_per_seq) == tiles_per_seq - 1
        zero = jnp.zeros(ap_ref.shape, ap_ref.dtype)
        ext_ref[0:HALO, :] = jnp.where(first, zero, ap_ref[...])
        ext_ref[HALO:HALO + tm, :] = a_ref[...]
        ext_ref[HALO + tm:, :] = jnp.where(last, zero, an_ref[...])

    a = ext_ref[...]
    rows = min(EPI_ROWS, tm)

    def conv(u, r0, cw_ref):
        w = u[HALO + r0 - 8:HALO + r0 + rows + 8]
        n = w.shape[0]
        cu = (pltpu.roll(w, 1, 0) * cw_ref[0:1, :] + w * cw_ref[1:2, :]
              + pltpu.roll(w, n - 1, 0) * cw_ref[2:3, :])
        return cu[8:8 + rows]

    ug = jnp.dot(a, wg_ref[...].astype(BF16), preferred_element_type=F32)
    uv = jnp.dot(a, wv_ref[...].astype(BF16), preferred_element_type=F32)
    for r0 in range(0, tm, rows):
        gate = conv(ug, r0, cwg_ref)
        val = conv(uv, r0, cwv_ref)
        o_ref[r0:r0 + rows, :] = (gate * jax.nn.sigmoid(gate) * val).astype(o_ref.dtype)


def _ffn_up_call(a, w_up, layer, conv_w, seq, w_down=None):
    m, kdim = a.shape
    f = w_up.shape[2] // 2
    tm = min(1024, seq)
    tn = _pick(f, (256, 128))
    tiles_per_seq = seq // tm
    fb = f // tn
    hb = tm // HALO
    n_halo = m // HALO
    n_steps = (m // tm) * fb
    cw = jnp.pad(conv_w, ((0, 8 - CONV_W), (0, 0)))
    in_specs = [pl.BlockSpec((tm, kdim), lambda i, j: (i, 0)),
                pl.BlockSpec((HALO, kdim), lambda i, j: (jnp.maximum(i * hb - 1, 0), 0)),
                pl.BlockSpec((HALO, kdim), lambda i, j: (jnp.minimum((i + 1) * hb, n_halo - 1), 0)),
                pl.BlockSpec((None, kdim, tn), lambda i, j: (layer, 0, j)),
                pl.BlockSpec((None, kdim, tn), lambda i, j: (layer, 0, fb + j)),
                pl.BlockSpec((8, tn), lambda i, j: (0, j)),
                pl.BlockSpec((8, tn), lambda i, j: (0, fb + j))]
    out_shape = jax.ShapeDtypeStruct((m, f), BF16)
    out_specs = pl.BlockSpec((tm, tn), lambda i, j: (i, j))
    args = [a, a, a, w_up, w_up, cw, cw]
    cast_down = w_down is not None
    if cast_down:
        fd, dd = w_down.shape[1:]
        slab = fd // n_steps
        assert fd % n_steps == 0 and slab % HALO == 0
        in_specs.append(pl.BlockSpec((None, slab, dd), lambda i, j: (layer, i * fb + j, 0)))
        out_shape = (out_shape, jax.ShapeDtypeStruct((fd, dd), BF16))
        out_specs = (out_specs, pl.BlockSpec((slab, dd), lambda i, j: (i * fb + j, 0)))
        args.append(w_down)
    return pl.pallas_call(
        functools.partial(_ffn_up_kernel, tiles_per_seq=tiles_per_seq, cast_down=cast_down),
        out_shape=out_shape,
        grid=(m // tm, fb),
        in_specs=in_specs,
        out_specs=out_specs,
        scratch_shapes=[pltpu.VMEM((tm + 2 * HALO, kdim), BF16)],
        compiler_params=_cparams(("parallel", "arbitrary")),
        name="ffn_up",
    )(*args)


def kernel(x, c, ctx, c_ctx, w_ada, b_ada, g_attn_pre, g_attn_post, g_mlp_pre, g_mlp_post,
           w_in, w_out, g_q_a, g_k_a, lam_q1, lam_k1, lam_q2, lam_k2, g_sub_b, sink_c, na_bias,
           w_up, conv_ffn_w, w_down):
    batch, seq, d = x.shape
    n_ctx = ctx.shape[1]
    depth = w_ada.shape[0]
    assert batch + 1 <= 8 and d == N_Q

    cond = jnp.zeros((8, d), F32).at[:batch].set(c).at[batch].set(c_ctx)
    mod_all = _ada_call(cond, w_ada, b_ada)
    rope_hd = _rope_tables(seq, HEAD_DIM)
    rope_dh = _rope_tables(seq, DH_B)
    scale_hd = HEAD_DIM ** -0.5
    scale_dh = DH_B ** -0.5

    def latent_mod(l):
        mod = mod_all[l, :batch].reshape(batch, 6, d)
        return tuple(mod[:, k] for k in range(6))

    h = ctx
    sh_a, sc_a, gt_a, sh_m, sc_m, gt_m = latent_mod(0)
    hx = _norm_mod_call(x, g_attn_pre[0], sc_a, sh_a)
    for l in range(depth):
        last = l == depth - 1
        lam_init = 0.8 - 0.6 * math.exp(-0.3 * l)
        sh_a, sc_a, gt_a, sh_m, sc_m, gt_m = latent_mod(l)
        mod_c = mod_all[l, batch].reshape(6, d)
        one = lambda v: v.reshape(1, d)
        lam_rows = jnp.zeros((8, LANE), F32).at[:4, :DH_B].set(
            jnp.stack([lam_q1[l], lam_k1[l], lam_q2[l], lam_k2[l]]))

        hx = hx.reshape(batch * seq, d)
        hc = _norm_mod_call(h.reshape(1, batch * n_ctx, d), g_attn_pre[l],
                            one(mod_c[1]), one(mod_c[0])).reshape(batch * n_ctx, d)
        p = _matmul_ws_call(hx, w_in, l, BF16)
        pc_off = 32 if last else 0
        pc = _matmul_ws_call(hc, w_in, l, BF16, col0=pc_off * LANE)

        dims = dict(batch=batch, seq=seq, ctx=n_ctx)
        q_c = _prep_call(p, QC_COL, GROUP_W, seq, rope=rope_hd, scale=scale_hd)
        k_c = _prep_call(p, KC_COL, KV_GQA * LANE, seq, rope=rope_hd)
        q_a = _prep_call(p, QA_COL, GROUP_W, seq, gain=g_q_a[l], rope=rope_hd,
                         scale=scale_hd * LOG2E)
        o_a = _flash_call(p, pc, pc_off, QA_COL, KA_COL, VA_COL, rope_hd, 1.0, diff=False,
                          q_ready=q_a, g_k=g_k_a[l], **dims)
        o_b = _flash_call(p, pc, pc_off, QB_COL, KB_COL, VB_COL, rope_dh, scale_dh * LOG2E,
                          diff=True, lam_rows=lam_rows, g_sub=g_sub_b[l], lam_init=lam_init, **dims)
        o_c = _band_call(sink_c[l], q_c, k_c, p, pc, pc_off, **dims)
        o_d = _nbr_call(p, pc, pc_off, na_bias[l], **dims)
        o = jnp.concatenate([o_a, o_b, o_c, o_d], axis=-1)
        y = _matmul_ws_call(o, w_out, l, BF16)
        x, hx = _norm_res_mod_call(x, y, g_attn_post[l], gt_a, g_mlp_pre[l], sc_m, sh_m)

        g, wd = _ffn_up_call(hx.reshape(batch * seq, d), w_up, l, conv_ffn_w[l], seq, w_down)
        y = _matmul_call(g, wd, BF16)
        if last:
            x = _norm_res_call(x, y, g_mlp_post[l], gt_m)
        else:
            nxt = latent_mod(l + 1)
            x, hx = _norm_res_mod_call(x, y, g_mlp_post[l], gt_m, g_attn_pre[l + 1], nxt[1], nxt[0])

        if not last:
            cdims = dict(batch=batch, ctx=n_ctx)
            qc_a = _prep_call(pc, QA_COL, GROUP_W, n_ctx, gain=g_q_a[l], scale=scale_hd)
            kc_a = _prep_call(pc, KA_COL, KV_GQA * LANE, n_ctx, gain=g_k_a[l])
            co_a = _ctx_call(qc_a, 0, kc_a, 0, pc, VA_COL, n_heads=KV_GQA, n_stack=GQA_G, **cdims)
            co_b = _ctx_call(pc, QB_COL, pc, KB_COL, pc, VB_COL, n_heads=N_GROUP_HEADS,
                             n_stack=2, diff=True, scale=scale_dh, lam_rows=lam_rows,
                             g_sub=g_sub_b[l], lam_init=lam_init, **cdims)
            co_c = _ctx_call(pc, QC_COL, pc, KC_COL, pc, VC_COL, n_heads=KV_GQA, n_stack=GQA_G,
                             scale=scale_hd, sink=sink_c[l], **cdims)
            co_d = _ctx_call(pc, QD_COL, pc, KD_COL, pc, VD_COL, n_heads=N_GROUP_HEADS,
                             n_stack=1, scale=scale_hd, **cdims)
            co = jnp.concatenate([co_a, co_b, co_c, co_d], axis=-1)
            yc = _matmul_ws_call(co, w_out, l, BF16)
            h3 = h.reshape(1, batch * n_ctx, d)
            h3 = _norm_res_call(h3, yc, g_attn_post[l], one(mod_c[2]))
            hc = _norm_mod_call(h3, g_mlp_pre[l], one(mod_c[4]), one(mod_c[3]))
            gc = _ffn_up_call(hc.reshape(batch * n_ctx, d), w_up, l, conv_ffn_w[l], n_ctx)
            yc = _matmul_call(gc, wd, BF16)
            h = _norm_res_call(h3, yc, g_mlp_post[l], one(mod_c[5])).reshape(batch, n_ctx, d)
    return x
```

```python
import functools
import math

import jax
import jax.numpy as jnp
import numpy as np
from jax import lax
from jax.experimental import pallas as pl
from jax.experimental.pallas import tpu as pltpu

F32 = jnp.float32
BF16 = jnp.bfloat16

GRID_W = 64
HEAD_DIM = 128
N_GROUP_HEADS = 8
KV_GQA = 2
GQA_G = N_GROUP_HEADS // KV_GQA
DH_B = HEAD_DIM // 2
WIN_C = 128
NA_KH = 8
NA_KW = 16
NBR_GROUP = 8
NBR_SLAB = 16
CONV_W = 3
ROPE_THETA = 10000.0
EPS = 1e-6
NEG_INF = -1e30
LOG2E = math.log2(math.e)
GROUP_W = N_GROUP_HEADS * HEAD_DIM
N_Q = 4 * GROUP_W

QA_COL, QB_COL, QC_COL, QD_COL = 0, 8, 16, 24
KA_COL, VA_COL = 32, 34
KB_COL, VB_COL = 36, 44
KC_COL, VC_COL = 52, 54
KD_COL, VD_COL = 56, 64
KV_COLS = 40

LANE = 128
HALO = 16
EPI_ROWS = 128
VMEM_LIMIT = 56 * 1024 * 1024


def _cparams(sem):
    return pltpu.CompilerParams(dimension_semantics=sem, vmem_limit_bytes=VMEM_LIMIT)


def _ada_kernel(a_ref, w_ref, b_ref, o_ref):
    a = a_ref[...]
    a = a * jax.nn.sigmoid(a)
    o_ref[...] = jnp.dot(a.astype(BF16), w_ref[...].astype(BF16),
                         preferred_element_type=F32) + b_ref[...]


def _ada_call(cond, w_ada, b_ada):
    n_layers, d, n = w_ada.shape
    tn = 512
    return pl.pallas_call(
        _ada_kernel,
        out_shape=jax.ShapeDtypeStruct((n_layers, 8, n), F32),
        grid=(n_layers, n // tn),
        in_specs=[pl.BlockSpec((8, d), lambda l, j: (0, 0)),
                  pl.BlockSpec((None, d, tn), lambda l, j: (l, 0, j)),
                  pl.BlockSpec((None, 1, tn), lambda l, j: (l, 0, j))],
        out_specs=pl.BlockSpec((None, 8, tn), lambda l, j: (l, 0, j)),
        compiler_params=_cparams(("parallel", "parallel")),
        name="ada_mod",
    )(cond, w_ada, b_ada.reshape(n_layers, 1, n))


def _rms(x, g):
    return x * lax.rsqrt(jnp.mean(x * x, axis=-1, keepdims=True) + EPS) * g


def _norm_mod_kernel(x_ref, g_ref, sc_ref, sh_ref, o_ref):
    y = _rms(x_ref[...], g_ref[...])
    o_ref[...] = (y * (1.0 + sc_ref[...]) + sh_ref[...]).astype(o_ref.dtype)


def _norm_mod_call(x, g, sc, sh):
    nb, r, d = x.shape
    tr = min(256, r)
    return pl.pallas_call(
        _norm_mod_kernel,
        out_shape=jax.ShapeDtypeStruct((nb, r, d), BF16),
        grid=(nb, r // tr),
        in_specs=[pl.BlockSpec((None, tr, d), lambda b, i: (b, i, 0)),
                  pl.BlockSpec((1, d), lambda b, i: (0, 0)),
                  pl.BlockSpec((None, 1, d), lambda b, i: (b, 0, 0)),
                  pl.BlockSpec((None, 1, d), lambda b, i: (b, 0, 0))],
        out_specs=pl.BlockSpec((None, tr, d), lambda b, i: (b, i, 0)),
        compiler_params=_cparams(("parallel", "parallel")),
        name="norm_mod",
    )(x, g.reshape(1, d), sc.reshape(nb, 1, d), sh.reshape(nb, 1, d))


def _norm_res_kernel(x_ref, y_ref, g_ref, gt_ref, o_ref):
    o_ref[...] = x_ref[...] + gt_ref[...] * _rms(y_ref[...].astype(F32), g_ref[...])


def _norm_res_call(x, y, g, gate):
    nb, r, d = x.shape
    tr = min(256, r)
    return pl.pallas_call(
        _norm_res_kernel,
        out_shape=jax.ShapeDtypeStruct((nb, r, d), F32),
        grid=(nb, r // tr),
        in_specs=[pl.BlockSpec((None, tr, d), lambda b, i: (b, i, 0)),
                  pl.BlockSpec((None, tr, d), lambda b, i: (b, i, 0)),
                  pl.BlockSpec((1, d), lambda b, i: (0, 0)),
                  pl.BlockSpec((None, 1, d), lambda b, i: (b, 0, 0))],
        out_specs=pl.BlockSpec((None, tr, d), lambda b, i: (b, i, 0)),
        compiler_params=_cparams(("parallel", "parallel")),
        name="norm_res",
    )(x, y.reshape(nb, r, d), g.reshape(1, d), gate.reshape(nb, 1, d))


def _norm_res_mod_kernel(x_ref, y_ref, g_ref, gt_ref, g2_ref, sc_ref, sh_ref, o_ref, h_ref):
    x = x_ref[...] + gt_ref[...] * _rms(y_ref[...].astype(F32), g_ref[...])
    o_ref[...] = x
    h_ref[...] = (_rms(x, g2_ref[...]) * (1.0 + sc_ref[...]) + sh_ref[...]).astype(h_ref.dtype)


def _norm_res_mod_call(x, y, g, gate, g_next, sc, sh):
    nb, r, d = x.shape
    tr = min(256, r)
    row = pl.BlockSpec((None, tr, d), lambda b, i: (b, i, 0))
    vec = pl.BlockSpec((None, 1, d), lambda b, i: (b, 0, 0))
    gain = pl.BlockSpec((1, d), lambda b, i: (0, 0))
    return pl.pallas_call(
        _norm_res_mod_kernel,
        out_shape=(jax.ShapeDtypeStruct((nb, r, d), F32), jax.ShapeDtypeStruct((nb, r, d), BF16)),
        grid=(nb, r // tr),
        in_specs=[row, row, gain, vec, gain, vec, vec],
        out_specs=(row, row),
        compiler_params=_cparams(("parallel", "parallel")),
        name="norm_res_mod",
    )(x, y.reshape(nb, r, d), g.reshape(1, d), gate.reshape(nb, 1, d), g_next.reshape(1, d),
      sc.reshape(nb, 1, d), sh.reshape(nb, 1, d))


def _matmul_kernel(a_ref, w_ref, o_ref, acc_ref, *, nk):
    k = pl.program_id(2)
    part = jnp.dot(a_ref[...], w_ref[...], preferred_element_type=F32)
    if nk == 1:
        o_ref[...] = part.astype(o_ref.dtype)
    else:
        @pl.when(k == 0)
        def _():
            acc_ref[...] = part

        @pl.when(jnp.logical_and(k > 0, k < nk - 1))
        def _():
            acc_ref[...] += part

        @pl.when(k == nk - 1)
        def _():
            o_ref[...] = (acc_ref[...] + part).astype(o_ref.dtype)


def _pick(n, prefs):
    for t in prefs:
        if n % t == 0:
            return t
    return n


def _matmul_call(a, w, out_dtype):
    m, kdim = a.shape
    _, n = w.shape
    tm = _pick(m, (1024, 512, 256) if kdim <= 4096 else (512, 256))
    tn = _pick(n, (512, 256, 128))
    operand_bytes = kdim * (tm + tn) * 2 * 2
    tk = kdim if operand_bytes <= VMEM_LIMIT * 6 // 7 else _pick(kdim, (5504, 4096, 2048))
    nk = kdim // tk
    return pl.pallas_call(
        functools.partial(_matmul_kernel, nk=nk),
        out_shape=jax.ShapeDtypeStruct((m, n), out_dtype),
        grid=(m // tm, n // tn, nk),
        in_specs=[pl.BlockSpec((tm, tk), lambda i, j, k: (i, k)),
                  pl.BlockSpec((tk, tn), lambda i, j, k: (k, j))],
        out_specs=pl.BlockSpec((tm, tn), lambda i, j, k: (i, j)),
        scratch_shapes=[pltpu.VMEM((tm, tn) if nk > 1 else (8, LANE), F32)],
        compiler_params=_cparams(("parallel", "parallel", "arbitrary")),
        name="matmul",
    )(a, w)


def _matmul_ws_kernel(a_ref, w_ref, o_ref, wb_ref):
    @pl.when(pl.program_id(1) == 0)
    def _():
        wb_ref[...] = w_ref[...].astype(BF16)

    o_ref[...] = jnp.dot(a_ref[...], wb_ref[...], preferred_element_type=F32).astype(o_ref.dtype)


def _matmul_ws_call(a, w, layer, out_dtype, col0=0):
    m, kdim = a.shape
    n = w.shape[2] - col0
    tm = _pick(m, (1024, 512, 256))
    tn = _pick(n, (512, 256, 128))
    assert col0 % tn == 0
    cb0 = col0 // tn
    return pl.pallas_call(
        _matmul_ws_kernel,
        out_shape=jax.ShapeDtypeStruct((m, n), out_dtype),
        grid=(n // tn, m // tm),
        in_specs=[pl.BlockSpec((tm, kdim), lambda j, i: (i, 0)),
                  pl.BlockSpec((None, kdim, tn), lambda j, i: (layer, 0, cb0 + j))],
        out_specs=pl.BlockSpec((tm, tn), lambda j, i: (i, j)),
        scratch_shapes=[pltpu.VMEM((kdim, tn), BF16)],
        compiler_params=_cparams(("parallel", "arbitrary")),
        name="matmul_ws",
    )(a, w)


def _swap_pairs(y):
    lane = lax.broadcasted_iota(jnp.int32, y.shape, 1)
    nxt = pltpu.roll(y, LANE - 1, 1)
    prv = pltpu.roll(y, 1, 1)
    return jnp.where((lane & 1) == 0, nxt, prv)


def _prep_tile(y, gain, cos, sin, scale):
    y = y.astype(F32)
    if gain is not None:
        y = _rms(y, gain)
    if cos is not None:
        y = y * cos + _swap_pairs(y) * sin
    if scale != 1.0:
        y = y * scale
    return y.astype(BF16)


def _prep_kernel(p_ref, cos_ref, sin_ref, g_ref, o_ref, *, n_chunks, norm, rope, scale):
    for c in range(n_chunks):
        sl = slice(c * LANE, (c + 1) * LANE)
        o_ref[:, sl] = _prep_tile(p_ref[:, sl], g_ref[...] if norm else None,
                                  cos_ref[...] if rope else None,
                                  sin_ref[...] if rope else None, scale)


def _prep_call(p, col0, width, seq, *, gain=None, rope=None, scale=1.0):
    rows = p.shape[0]
    tr = min(256, seq)
    n_seq_tiles = seq // tr
    bw = min(width, 512)
    assert (col0 * LANE) % bw == 0 and width % bw == 0
    cb0 = col0 * LANE // bw
    if rope is None:
        cos = sin = jnp.zeros((tr, LANE), F32)
        tab_map = lambda i, j: (0, 0)
    else:
        cos, sin = rope
        tab_map = lambda i, j: (i % n_seq_tiles, 0)
    g = (jnp.ones((HEAD_DIM,), F32) if gain is None else gain).reshape(1, HEAD_DIM)
    kern = functools.partial(_prep_kernel, n_chunks=bw // LANE, norm=gain is not None,
                             rope=rope is not None, scale=float(scale))
    return pl.pallas_call(
        kern,
        out_shape=jax.ShapeDtypeStruct((rows, width), BF16),
        grid=(rows // tr, width // bw),
        in_specs=[pl.BlockSpec((tr, bw), lambda i, j: (i, cb0 + j)),
                  pl.BlockSpec((tr, LANE), tab_map),
                  pl.BlockSpec((tr, LANE), tab_map),
                  pl.BlockSpec((1, HEAD_DIM), lambda i, j: (0, 0))],
        out_specs=pl.BlockSpec((tr, bw), lambda i, j: (i, j)),
        compiler_params=_cparams(("parallel", "parallel")),
        name="qk_prep",
    )(p, cos, sin, g)


def _rope_tables(seq, head_dim):
    t = np.arange(seq)
    rows = (t // GRID_W).astype(np.float32)
    cols = (t % GRID_W).astype(np.float32)
    quarter = head_dim // 4
    inv = (ROPE_THETA ** (-np.arange(quarter, dtype=np.float32) / quarter)).astype(np.float32)
    ang = np.concatenate([rows[:, None] * inv, cols[:, None] * inv], axis=-1)
    lane = np.arange(LANE)
    pair = (lane % head_dim) // 2
    sign = np.where(lane % 2 == 0, -1.0, 1.0).astype(np.float32)
    cos = np.cos(ang).astype(np.float32)[:, pair]
    sin = np.sin(ang).astype(np.float32)[:, pair] * sign
    return jnp.asarray(cos), jnp.asarray(sin)


def _qk(q, k):
    return lax.dot_general(q, k, (((1,), (1,)), ((), ())), preferred_element_type=F32)


def _stack_heads(q, n):
    if n == 1:
        return q
    return jnp.concatenate([q[:, g * LANE:(g + 1) * LANE] for g in range(n)], axis=0)


def _unstack_heads(o, n):
    if n == 1:
        return o
    t = o.shape[0] // n
    return jnp.concatenate([o[g * t:(g + 1) * t] for g in range(n)], axis=1)


def _stack_diff(q):
    lane = lax.broadcasted_iota(jnp.int32, q.shape, 1)
    zero = jnp.zeros_like(q)
    return jnp.concatenate([jnp.where(lane < DH_B, q, zero),
                            jnp.where(lane >= DH_B, q, zero)], axis=0)


def _softmax_attend(groups, sink=None):
    m = None
    for s, _ in groups:
        sm = jnp.max(s, axis=-1, keepdims=True)
        m = sm if m is None else jnp.maximum(m, sm)
    if sink is not None:
        m = jnp.maximum(m, sink)
    l = None
    acc = None
    for s, v in groups:
        p = jnp.exp(s - m)
        ps = jnp.sum(p, axis=-1, keepdims=True)
        pv = jnp.dot(p.astype(v.dtype), v, preferred_element_type=F32)
        l = ps if l is None else l + ps
        acc = pv if acc is None else acc + pv
    if sink is not None:
        l = l + jnp.exp(sink - m)
    return acc / l


def _diff_lambda(lam_ref, lam_init):
    a = jnp.sum(lam_ref[0:1, :] * lam_ref[1:2, :], axis=-1, keepdims=True)
    b = jnp.sum(lam_ref[2:3, :] * lam_ref[3:4, :], axis=-1, keepdims=True)
    return jnp.exp(a) - jnp.exp(b) + lam_init


def _diff_finalize(o, lam_ref, gsub_ref, lam_init):
    t = o.shape[0] // 2
    d = o[:t] - _diff_lambda(lam_ref, lam_init) * o[t:]
    return _rms(d, gsub_ref[...]) * (1.0 - lam_init)


def _sink_rows(sink_ref, kv, rows_per_head, n_rows):
    head = lax.broadcasted_iota(jnp.int32, (n_rows, 1), 0) // rows_per_head
    out = jnp.zeros((n_rows, 1), F32)
    for g in range(GQA_G):
        out = jnp.where(head == g, sink_ref[kv * GQA_G + g], out)
    return out


def _lane_tiles(s):
    return [s[:, j * LANE:(j + 1) * LANE] for j in range(s.shape[1] // LANE)]


def _flash_kernel(q_ref, kl_ref, vl_ref, kc_ref, vc_ref, cq_ref, sq_ref, ck_ref, sk_ref,
                  gq_ref, gk_ref, lam_ref, gsub_ref, o_ref, kp_ref, kcp_ref, *,
                  diff, n_stack, tk, lam_init, norm, q_scale, q_ready):
    gq = gq_ref[...] if norm else None
    gk = gk_ref[...] if norm else None

    @pl.when(pl.program_id(2) == 0)
    def _():
        for c in range(kl_ref.shape[0] // tk):
            sl = slice(c * tk, (c + 1) * tk)
            kp_ref[sl, :] = _prep_tile(kl_ref[sl, :], gk, ck_ref[sl, :], sk_ref[sl, :], 1.0)
        kcp_ref[...] = _prep_tile(kc_ref[...], gk, None, None, 1.0)

    kl_ref, kc_ref = kp_ref, kcp_ref
    if q_ready:
        q = q_ref[...]
    else:
        cq, sq = cq_ref[...], sq_ref[...]
        q = jnp.concatenate([_prep_tile(q_ref[:, g * LANE:(g + 1) * LANE], gq, cq, sq, q_scale)
                             for g in range(q_ref.shape[1] // LANE)], axis=1)
    q = _stack_diff(q) if diff else _stack_heads(q, n_stack)

    def row_max(tiles):
        m = tiles[0]
        for t in tiles[1:]:
            m = jnp.maximum(m, t)
        return jnp.broadcast_to(jnp.max(m, axis=-1, keepdims=True), m.shape)

    def weights(tiles, m, v):
        ps = [jnp.exp2(t - m) for t in tiles]
        l = ps[0]
        for p in ps[1:]:
            l = l + p
        pv = jnp.dot(jnp.concatenate(ps, axis=1).astype(v.dtype), v, preferred_element_type=F32)
        return l, pv

    def latent_scores(c):
        return _lane_tiles(_qk(q, kl_ref[c * tk:(c + 1) * tk, :]))

    n_chunks = kl_ref.shape[0] // tk
    tiles = _lane_tiles(_qk(q, kc_ref[...]))
    nxt = latent_scores(0)
    m = row_max(tiles)
    l, acc = weights(tiles, m, vc_ref[...])
    for c in range(n_chunks):
        tiles = nxt
        if c + 1 < n_chunks:
            nxt = latent_scores(c + 1)
        m_new = jnp.maximum(m, row_max(tiles))
        alpha = jnp.exp2(m - m_new)
        l_c, pv = weights(tiles, m_new, vl_ref[c * tk:(c + 1) * tk, :])
        l = alpha * l + l_c
        acc = alpha * acc + pv
        m = m_new
    o = acc / jnp.sum(l, axis=-1, keepdims=True)
    if diff:
        o = _diff_finalize(o, lam_ref, gsub_ref, lam_init)
    else:
        o = _unstack_heads(o, n_stack)
    o_ref[...] = o.astype(o_ref.dtype)


def _flash_call(p, pc, pc_off, q_col, k_col, v_col, rope, q_scale, *, batch, seq, ctx, diff,
                q_ready=None, g_q=None, g_k=None, lam_rows=None, g_sub=None, lam_init=0.0):
    n_heads = N_GROUP_HEADS if diff else KV_GQA
    n_stack = 2 if diff else GQA_G
    tq = min(1024 // n_stack, seq)
    qw = LANE if diff else GQA_G * LANE
    qb = 0 if q_ready is not None else q_col * LANE // qw
    nq = seq // tq
    norm = g_k is not None
    ones = jnp.ones((HEAD_DIM,), F32)
    if lam_rows is None:
        lam_rows = jnp.zeros((8, LANE), F32)
    vec = lambda g: (ones if g is None else g).reshape(1, HEAD_DIM)
    cos, sin = rope
    kern = functools.partial(_flash_kernel, diff=diff, n_stack=n_stack, tk=min(1024, seq),
                             lam_init=float(lam_init), norm=norm, q_scale=float(q_scale),
                             q_ready=q_ready is not None)
    const = lambda b, h, i: (0, 0)
    return pl.pallas_call(
        kern,
        out_shape=jax.ShapeDtypeStruct((batch * seq, GROUP_W), BF16),
        grid=(batch, n_heads, nq),
        in_specs=[pl.BlockSpec((tq, qw), lambda b, h, i: (b * nq + i, qb + h)),
                  pl.BlockSpec((seq, LANE), lambda b, h, i: (b, k_col + h)),
                  pl.BlockSpec((seq, LANE), lambda b, h, i: (b, v_col + h)),
                  pl.BlockSpec((ctx, LANE), lambda b, h, i: (b, k_col - pc_off + h)),
                  pl.BlockSpec((ctx, LANE), lambda b, h, i: (b, v_col - pc_off + h)),
                  pl.BlockSpec((tq, LANE), lambda b, h, i: (i, 0)),
                  pl.BlockSpec((tq, LANE), lambda b, h, i: (i, 0)),
                  pl.BlockSpec((seq, LANE), const),
                  pl.BlockSpec((seq, LANE), const),
                  pl.BlockSpec((1, HEAD_DIM), const),
                  pl.BlockSpec((1, HEAD_DIM), const),
                  pl.BlockSpec((8, LANE), const),
                  pl.BlockSpec((1, HEAD_DIM), const)],
        out_specs=pl.BlockSpec((tq, qw), lambda b, h, i: (b * nq + i, h)),
        scratch_shapes=[pltpu.VMEM((seq, LANE), BF16), pltpu.VMEM((ctx, LANE), BF16)],
        compiler_params=_cparams(("parallel", "parallel", "arbitrary")),
        name="attn_full",
    )(p if q_ready is None else q_ready, p, p, pc, pc, cos, sin, cos, sin, vec(g_q), vec(g_k),
      lam_rows, vec(g_sub))


def _band_kernel(sink_ref, q_ref, kp_ref, kx_ref, kn_ref, vp_ref, vx_ref, vn_ref,
                 kc_ref, vc_ref, o_ref, *, tq, seq):
    kv = pl.program_id(1)
    i = pl.program_id(2)
    q = _stack_heads(q_ref[...], GQA_G)
    rows = q.shape[0]
    k_loc = jnp.concatenate([kp_ref[...], kx_ref[...], kn_ref[...]], axis=0)
    v_loc = jnp.concatenate([vp_ref[...], vx_ref[...], vn_ref[...]], axis=0)
    s_loc = _qk(q, k_loc)
    qpos = i * tq + (lax.broadcasted_iota(jnp.int32, s_loc.shape, 0) & (tq - 1))
    kpos = (i - 1) * tq + lax.broadcasted_iota(jnp.int32, s_loc.shape, 1)
    ok = jnp.where(kpos >= 0, jnp.abs(qpos - kpos), WIN_C + 1)
    ok = jnp.where(kpos < seq, ok, WIN_C + 1) <= WIN_C
    s_loc = jnp.where(ok, s_loc, NEG_INF)
    s_ctx = _qk(q, kc_ref[...])
    sink = _sink_rows(sink_ref, kv, tq, rows)
    o = _softmax_attend([(s_loc, v_loc), (s_ctx, vc_ref[...])], sink)
    o_ref[...] = _unstack_heads(o, GQA_G).astype(o_ref.dtype)


def _band_call(sink, q, kl, p, pc, pc_off, *, batch, seq, ctx):
    tq = WIN_C
    nq = seq // tq
    qw = GQA_G * LANE

    def nb(delta, col):
        def imap(b, h, i):
            return (b * nq + jnp.clip(i + delta, 0, nq - 1), col + h)
        return imap

    return pl.pallas_call(
        functools.partial(_band_kernel, tq=tq, seq=seq),
        out_shape=jax.ShapeDtypeStruct((batch * seq, GROUP_W), BF16),
        grid=(batch, KV_GQA, nq),
        in_specs=[pl.BlockSpec(memory_space=pltpu.SMEM),
                  pl.BlockSpec((tq, qw), lambda b, h, i: (b * nq + i, h)),
                  pl.BlockSpec((tq, LANE), nb(-1, 0)),
                  pl.BlockSpec((tq, LANE), nb(0, 0)),
                  pl.BlockSpec((tq, LANE), nb(1, 0)),
                  pl.BlockSpec((tq, LANE), nb(-1, VC_COL)),
                  pl.BlockSpec((tq, LANE), nb(0, VC_COL)),
                  pl.BlockSpec((tq, LANE), nb(1, VC_COL)),
                  pl.BlockSpec((ctx, LANE), lambda b, h, i: (b, KC_COL - pc_off + h)),
                  pl.BlockSpec((ctx, LANE), lambda b, h, i: (b, VC_COL - pc_off + h))],
        out_specs=pl.BlockSpec((tq, qw), lambda b, h, i: (b * nq + i, h)),
        compiler_params=_cparams(("parallel", "parallel", "parallel")),
        name="attn_band",
    )(sink, q, kl, kl, kl, p, p, p, pc, pc)


def _nbr_build_bias(tab_ref, bias_ref):
    shape = (GRID_W, LANE)
    w = lax.broadcasted_iota(jnp.int32, shape, 0)
    lane = lax.broadcasted_iota(jnp.int32, shape, 1)
    c = lane & (GRID_W - 1)
    cs = jnp.clip(w - NA_KW // 2, 0, GRID_W - NA_KW)
    d = c - cs
    mask = jnp.where(d >= 0, jnp.where(d < NA_KW, 0.0, NEG_INF), NEG_INF).astype(F32)
    neg = jnp.full(shape, NEG_INF, F32)
    toep = [pltpu.roll(jnp.broadcast_to(tab_ref[ro:ro + 1, :], shape), LANE - (NA_KW - 1), 1,
                       stride=1, stride_axis=0) + mask for ro in range(2 * NA_KH - 1)]
    kinds = (lambda dr: (dr, max(dr - NA_KH // 2, 0)),
             lambda dr: (NA_KH // 2 + dr, dr),
             lambda dr: (NBR_GROUP + dr, min(NA_KH // 2 + dr, NBR_SLAB - NA_KH)))
    for kind, rel in enumerate(kinds):
        for dr in range(NBR_GROUP):
            r_rel, rs_rel = rel(dr)
            for j in range(NBR_SLAB // 2):
                halves = [toep[kr - r_rel + NA_KH - 1] if rs_rel <= kr < rs_rel + NA_KH else neg
                          for kr in (2 * j, 2 * j + 1)]
                tile = neg if halves[0] is neg and halves[1] is neg else jnp.where(
                    lane < GRID_W, halves[0], halves[1])
                bias_ref[kind, dr * GRID_W:(dr + 1) * GRID_W, j * LANE:(j + 1) * LANE] = tile


def _nbr_kernel(q_ref, k_ref, v_ref, kc_ref, vc_ref, tab_ref, o_ref, bias_ref, *, n_rows, scale):
    n_groups = n_rows // NBR_GROUP
    gq = NBR_GROUP * GRID_W
    slab = NBR_SLAB * GRID_W
    kc = kc_ref[...]
    vc = vc_ref[...]
    _nbr_build_bias(tab_ref, bias_ref)

    def keys_of(g):
        base = min(max(g * NBR_GROUP - NA_KH // 2, 0), n_rows - NBR_SLAB)
        return slice(base * GRID_W, base * GRID_W + slab)

    def scores(g):
        q = q_ref[g * gq:(g + 1) * gq, :]
        return _qk(q, k_ref[keys_of(g), :]), _qk(q, kc)

    nxt = scores(0)
    for g in range(n_groups):
        raw_loc, raw_ctx = nxt
        if g + 1 < n_groups:
            nxt = scores(g + 1)
        kind = 0 if g == 0 else (2 if g == n_groups - 1 else 1)
        s_loc = raw_loc * scale + bias_ref[kind]
        o = _softmax_attend([(s_loc, v_ref[keys_of(g), :]), (raw_ctx * scale, vc)])
        o_ref[g * gq:(g + 1) * gq, :] = o.astype(o_ref.dtype)


def _nbr_table(na_bias):
    h, nr, nc = na_bias.shape
    t = jnp.pad(na_bias.astype(F32), ((0, 0), (0, 16 - nr), (0, GRID_W - nc)))
    return jnp.concatenate([t] * (LANE // GRID_W), axis=-1)


def _nbr_call(p, pc, pc_off, na_bias, *, batch, seq, ctx):
    n_rows = seq // GRID_W
    assert n_rows >= NBR_SLAB and n_rows % NBR_GROUP == 0 and 2 * GRID_W == LANE
    assert NBR_SLAB >= NBR_GROUP + NA_KH - 1 and NBR_SLAB == NBR_GROUP + NA_KH
    scale = HEAD_DIM ** -0.5
    return pl.pallas_call(
        functools.partial(_nbr_kernel, n_rows=n_rows, scale=scale),
        out_shape=jax.ShapeDtypeStruct((batch * seq, GROUP_W), BF16),
        grid=(batch, N_GROUP_HEADS),
        in_specs=[pl.BlockSpec((seq, LANE), lambda b, h: (b, QD_COL + h)),
                  pl.BlockSpec((seq, LANE), lambda b, h: (b, KD_COL + h)),
                  pl.BlockSpec((seq, LANE), lambda b, h: (b, VD_COL + h)),
                  pl.BlockSpec((ctx, LANE), lambda b, h: (b, KD_COL - pc_off + h)),
                  pl.BlockSpec((ctx, LANE), lambda b, h: (b, VD_COL - pc_off + h)),
                  pl.BlockSpec((None, 16, LANE), lambda b, h: (h, 0, 0))],
        out_specs=pl.BlockSpec((seq, LANE), lambda b, h: (b, h)),
        scratch_shapes=[pltpu.VMEM((3, NBR_GROUP * GRID_W, NBR_SLAB * GRID_W), F32)],
        compiler_params=_cparams(("parallel", "parallel")),
        name="attn_nbr",
    )(p, p, p, pc, pc, _nbr_table(na_bias))


def _ctx_kernel(sink_ref, q_ref, k_ref, v_ref, lam_ref, gsub_ref, o_ref, *,
                diff, n_stack, scale, use_sink, lam_init):
    q = q_ref[...]
    q = _stack_diff(q) if diff else _stack_heads(q, n_stack)
    s = _qk(q, k_ref[...])
    if scale != 1.0:
        s = s * scale
    sink = None
    if use_sink:
        sink = _sink_rows(sink_ref, pl.program_id(1), q_ref.shape[0], q.shape[0])
    o = _softmax_attend([(s, v_ref[...])], sink)
    if diff:
        o = _diff_finalize(o, lam_ref, gsub_ref, lam_init)
    else:
        o = _unstack_heads(o, n_stack)
    o_ref[...] = o.astype(o_ref.dtype)


def _ctx_call(q, q_col, k, k_col, v, v_col, *, batch, ctx, n_heads, n_stack, diff=False,
              scale=1.0, sink=None, lam_rows=None, g_sub=None, lam_init=0.0):
    qw = LANE if diff else n_stack * LANE
    qb = q_col * LANE // qw
    if lam_rows is None:
        lam_rows = jnp.zeros((8, LANE), F32)
        g_sub = jnp.ones((HEAD_DIM,), F32)
    use_sink = sink is not None
    if sink is None:
        sink = jnp.zeros((N_GROUP_HEADS,), F32)
    kern = functools.partial(_ctx_kernel, diff=diff, n_stack=n_stack, scale=float(scale),
                             use_sink=use_sink, lam_init=float(lam_init))
    return pl.pallas_call(
        kern,
        out_shape=jax.ShapeDtypeStruct((batch * ctx, GROUP_W), BF16),
        grid=(batch, n_heads),
        in_specs=[pl.BlockSpec(memory_space=pltpu.SMEM),
                  pl.BlockSpec((ctx, qw), lambda b, h: (b, qb + h)),
                  pl.BlockSpec((ctx, LANE), lambda b, h: (b, k_col + h)),
                  pl.BlockSpec((ctx, LANE), lambda b, h: (b, v_col + h)),
                  pl.BlockSpec((8, LANE), lambda b, h: (0, 0)),
                  pl.BlockSpec((1, HEAD_DIM), lambda b, h: (0, 0))],
        out_specs=pl.BlockSpec((ctx, qw), lambda b, h: (b, h)),
        compiler_params=_cparams(("parallel", "parallel")),
        name="attn_ctx",
    )(sink, q, k, v, lam_rows, g_sub.reshape(1, HEAD_DIM))


def _ffn_up_kernel(a_ref, ap_ref, an_ref, wg_ref, wv_ref, cwg_ref, cwv_ref, *rest,
                   tiles_per_seq, cast_down):
    if cast_down:
        wd_ref, o_ref, wdb_ref, ext_ref = rest
        wdb_ref[...] = wd_ref[...].astype(BF16)
    else:
        o_ref, ext_ref = rest
    i = pl.program_id(0)
    tm = a_ref.shape[0]

    @pl.when(pl.program_id(1) == 0)
    def _():
        first = (i % tiles_per_seq) == 0
        last = (i % tiles_per_seq) == tiles_per_seq - 1
        zero = jnp.zeros(ap_ref.shape, ap_ref.dtype)
        ext_ref[0:HALO, :] = jnp.where(first, zero, ap_ref[...])
        ext_ref[HALO:HALO + tm, :] = a_ref[...]
        ext_ref[HALO + tm:, :] = jnp.where(last, zero, an_ref[...])

    a = ext_ref[...]
    rows = min(EPI_ROWS, tm)

    def conv(u, r0, cw_ref):
        w = u[HALO + r0 - 8:HALO + r0 + rows + 8]
        n = w.shape[0]
        cu = (pltpu.roll(w, 1, 0) * cw_ref[0:1, :] + w * cw_ref[1:2, :]
              + pltpu.roll(w, n - 1, 0) * cw_ref[2:3, :])
        return cu[8:8 + rows]

    ug = jnp.dot(a, wg_ref[...].astype(BF16), preferred_element_type=F32)
    uv = jnp.dot(a, wv_ref[...].astype(BF16), preferred_element_type=F32)
    for r0 in range(0, tm, rows):
        gate = conv(ug, r0, cwg_ref)
        val = conv(uv, r0, cwv_ref)
        o_ref[r0:r0 + rows, :] = (gate * jax.nn.sigmoid(gate) * val).astype(o_ref.dtype)


def _ffn_up_call(a, w_up, layer, conv_w, seq, w_down=None):
    m, kdim = a.shape
    f = w_up.shape[2] // 2
    tm = min(1024, seq)
    tn = _pick(f, (256, 128))
    tiles_per_seq = seq // tm
    fb = f // tn
    hb = tm // HALO
    n_halo = m // HALO
    n_steps = (m // tm) * fb
    cw = jnp.pad(conv_w, ((0, 8 - CONV_W), (0, 0)))
    in_specs = [pl.BlockSpec((tm, kdim), lambda i, j: (i, 0)),
                pl.BlockSpec((HALO, kdim), lambda i, j: (jnp.maximum(i * hb - 1, 0), 0)),
                pl.BlockSpec((HALO, kdim), lambda i, j: (jnp.minimum((i + 1) * hb, n_halo - 1), 0)),
                pl.BlockSpec((None, kdim, tn), lambda i, j: (layer, 0, j)),
                pl.BlockSpec((None, kdim, tn), lambda i, j: (layer, 0, fb + j)),
                pl.BlockSpec((8, tn), lambda i, j: (0, j)),
                pl.BlockSpec((8, tn), lambda i, j: (0, fb + j))]
    out_shape = jax.ShapeDtypeStruct((m, f), BF16)
    out_specs = pl.BlockSpec((tm, tn), lambda i, j: (i, j))
    args = [a, a, a, w_up, w_up, cw, cw]
    cast_down = w_down is not None
    if cast_down:
        fd, dd = w_down.shape[1:]
        slab = fd // n_steps
        assert fd % n_steps == 0 and slab % HALO == 0
        in_specs.append(pl.BlockSpec((None, slab, dd), lambda i, j: (layer, i * fb + j, 0)))
        out_shape = (out_shape, jax.ShapeDtypeStruct((fd, dd), BF16))
        out_specs = (out_specs, pl.BlockSpec((slab, dd), lambda i, j: (i * fb + j, 0)))
        args.append(w_down)
    return pl.pallas_call(
        functools.partial(_ffn_up_kernel, tiles_per_seq=tiles_per_seq, cast_down=cast_down),
        out_shape=out_shape,
        grid=(m // tm, fb),
        in_specs=in_specs,
        out_specs=out_specs,
        scratch_shapes=[pltpu.VMEM((tm + 2 * HALO, kdim), BF16)],
        compiler_params=_cparams(("parallel", "arbitrary")),
        name="ffn_up",
    )(*args)


def kernel(x, c, ctx, c_ctx, w_ada, b_ada, g_attn_pre, g_attn_post, g_mlp_pre, g_mlp_post,
           w_in, w_out, g_q_a, g_k_a, lam_q1, lam_k1, lam_q2, lam_k2, g_sub_b, sink_c, na_bias,
           w_up, conv_ffn_w, w_down):
    batch, seq, d = x.shape
    n_ctx = ctx.shape[1]
    depth = w_ada.shape[0]
    assert batch + 1 <= 8 and d == N_Q

    cond = jnp.zeros((8, d), F32).at[:batch].set(c).at[batch].set(c_ctx)
    mod_all = _ada_call(cond, w_ada, b_ada)
    rope_hd = _rope_tables(seq, HEAD_DIM)
    rope_dh = _rope_tables(seq, DH_B)
    scale_hd = HEAD_DIM ** -0.5
    scale_dh = DH_B ** -0.5

    def latent_mod(l):
        mod = mod_all[l, :batch].reshape(batch, 6, d)
        return tuple(mod[:, k] for k in range(6))

    h = ctx
    sh_a, sc_a, gt_a, sh_m, sc_m, gt_m = latent_mod(0)
    hx = _norm_mod_call(x, g_attn_pre[0], sc_a, sh_a)
    for l in range(depth):
        last = l == depth - 1
        lam_init = 0.8 - 0.6 * math.exp(-0.3 * l)
        sh_a, sc_a, gt_a, sh_m, sc_m, gt_m = latent_mod(l)
        mod_c = mod_all[l, batch].reshape(6, d)
        one = lambda v: v.reshape(1, d)
        lam_rows = jnp.zeros((8, LANE), F32).at[:4, :DH_B].set(
            jnp.stack([lam_q1[l], lam_k1[l], lam_q2[l], lam_k2[l]]))

        hx = hx.reshape(batch * seq, d)
        hc = _norm_mod_call(h.reshape(1, batch * n_ctx, d), g_attn_pre[l],
                            one(mod_c[1]), one(mod_c[0])).reshape(batch * n_ctx, d)
        p = _matmul_ws_call(hx, w_in, l, BF16)
        pc_off = 32 if last else 0
        pc = _matmul_ws_call(hc, w_in, l, BF16, col0=pc_off * LANE)

        dims = dict(batch=batch, seq=seq, ctx=n_ctx)
        q_c = _prep_call(p, QC_COL, GROUP_W, seq, rope=rope_hd, scale=scale_hd)
        k_c = _prep_call(p, KC_COL, KV_GQA * LANE, seq, rope=rope_hd)
        q_a = _prep_call(p, QA_COL, GROUP_W, seq, gain=g_q_a[l], rope=rope_hd,
                         scale=scale_hd * LOG2E)
        o_a = _flash_call(p, pc, pc_off, QA_COL, KA_COL, VA_COL, rope_hd, 1.0, diff=False,
                          q_ready=q_a, g_k=g_k_a[l], **dims)
        o_b = _flash_call(p, pc, pc_off, QB_COL, KB_COL, VB_COL, rope_dh, scale_dh * LOG2E,
                          diff=True, lam_rows=lam_rows, g_sub=g_sub_b[l], lam_init=lam_init, **dims)
        o_c = _band_call(sink_c[l], q_c, k_c, p, pc, pc_off, **dims)
        o_d = _nbr_call(p, pc, pc_off, na_bias[l], **dims)
        o = jnp.concatenate([o_a, o_b, o_c, o_d], axis=-1)
        y = _matmul_ws_call(o, w_out, l, BF16)
        x, hx = _norm_res_mod_call(x, y, g_attn_post[l], gt_a, g_mlp_pre[l], sc_m, sh_m)

        g, wd = _ffn_up_call(hx.reshape(batch * seq, d), w_up, l, conv_ffn_w[l], seq, w_down)
        y = _matmul_call(g, wd, BF16)
        if last:
            x = _norm_res_call(x, y, g_mlp_post[l], gt_m)
        else:
            nxt = latent_mod(l + 1)
            x, hx = _norm_res_mod_call(x, y, g_mlp_post[l], gt_m, g_attn_pre[l + 1], nxt[1], nxt[0])

        if not last:
            cdims = dict(batch=batch, ctx=n_ctx)
            qc_a = _prep_call(pc, QA_COL, GROUP_W, n_ctx, gain=g_q_a[l], scale=scale_hd)
            kc_a = _prep_call(pc, KA_COL, KV_GQA * LANE, n_ctx, gain=g_k_a[l])
            co_a = _ctx_call(qc_a, 0, kc_a, 0, pc, VA_COL, n_heads=KV_GQA, n_stack=GQA_G, **cdims)
            co_b = _ctx_call(pc, QB_COL, pc, KB_COL, pc, VB_COL, n_heads=N_GROUP_HEADS,
                             n_stack=2, diff=True, scale=scale_dh, lam_rows=lam_rows,
                             g_sub=g_sub_b[l], lam_init=lam_init, **cdims)
            co_c = _ctx_call(pc, QC_COL, pc, KC_COL, pc, VC_COL, n_heads=KV_GQA, n_stack=GQA_G,
                             scale=scale_hd, sink=sink_c[l], **cdims)
            co_d = _ctx_call(pc, QD_COL, pc, KD_COL, pc, VD_COL, n_heads=N_GROUP_HEADS,
                             n_stack=1, scale=scale_hd, **cdims)
            co = jnp.concatenate([co_a, co_b, co_c, co_d], axis=-1)
            yc = _matmul_ws_call(co, w_out, l, BF16)
            h3 = h.reshape(1, batch * n_ctx, d)
            h3 = _norm_res_call(h3, yc, g_attn_post[l], one(mod_c[2]))
            hc = _norm_mod_call(h3, g_mlp_pre[l], one(mod_c[4]), one(mod_c[3]))
            gc = _ffn_up_call(hc.reshape(batch * n_ctx, d), w_up, l, conv_ffn_w[l], n_ctx)
            yc = _matmul_call(gc, wd, BF16)
            h = _norm_res_call(h3, yc, g_mlp_post[l], one(mod_c[5])).reshape(batch, n_ctx, d)
    return x
```

```python
import functools
import math

import jax
import jax.numpy as jnp
import numpy as np
from jax import lax
from jax.experimental import pallas as pl
from jax.experimental.pallas import tpu as pltpu

F32 = jnp.float32
BF16 = jnp.bfloat16

GRID_W = 64
HEAD_DIM = 128
N_GROUP_HEADS = 8
KV_GQA = 2
GQA_G = N_GROUP_HEADS // KV_GQA
DH_B = HEAD_DIM // 2
WIN_C = 128
NA_KH = 8
NA_KW = 16
NBR_GROUP = 8
NBR_SLAB = 16
CONV_W = 3
ROPE_THETA = 10000.0
EPS = 1e-6
NEG_INF = -1e30
LOG2E = math.log2(math.e)
GROUP_W = N_GROUP_HEADS * HEAD_DIM
N_Q = 4 * GROUP_W

QA_COL, QB_COL, QC_COL, QD_COL = 0, 8, 16, 24
KA_COL, VA_COL = 32, 34
KB_COL, VB_COL = 36, 44
KC_COL, VC_COL = 52, 54
KD_COL, VD_COL = 56, 64
KV_COLS = 40

LANE = 128
HALO = 16
EPI_ROWS = 128
VMEM_LIMIT = 56 * 1024 * 1024


def _cparams(sem):
    return pltpu.CompilerParams(dimension_semantics=sem, vmem_limit_bytes=VMEM_LIMIT)


def _ada_kernel(a_ref, w_ref, b_ref, o_ref):
    a = a_ref[...]
    a = a * jax.nn.sigmoid(a)
    o_ref[...] = jnp.dot(a.astype(BF16), w_ref[...].astype(BF16),
                         preferred_element_type=F32) + b_ref[...]


def _ada_call(cond, w_ada, b_ada):
    n_layers, d, n = w_ada.shape
    tn = 512
    return pl.pallas_call(
        _ada_kernel,
        out_shape=jax.ShapeDtypeStruct((n_layers, 8, n), F32),
        grid=(n_layers, n // tn),
        in_specs=[pl.BlockSpec((8, d), lambda l, j: (0, 0)),
                  pl.BlockSpec((None, d, tn), lambda l, j: (l, 0, j)),
                  pl.BlockSpec((None, 1, tn), lambda l, j: (l, 0, j))],
        out_specs=pl.BlockSpec((None, 8, tn), lambda l, j: (l, 0, j)),
        compiler_params=_cparams(("parallel", "parallel")),
        name="ada_mod",
    )(cond, w_ada, b_ada.reshape(n_layers, 1, n))


def _rms(x, g):
    return x * lax.rsqrt(jnp.mean(x * x, axis=-1, keepdims=True) + EPS) * g


def _norm_mod_kernel(x_ref, g_ref, sc_ref, sh_ref, o_ref):
    y = _rms(x_ref[...], g_ref[...])
    o_ref[...] = (y * (1.0 + sc_ref[...]) + sh_ref[...]).astype(o_ref.dtype)


def _norm_mod_call(x, g, sc, sh):
    nb, r, d = x.shape
    tr = min(256, r)
    return pl.pallas_call(
        _norm_mod_kernel,
        out_shape=jax.ShapeDtypeStruct((nb, r, d), BF16),
        grid=(nb, r // tr),
        in_specs=[pl.BlockSpec((None, tr, d), lambda b, i: (b, i, 0)),
                  pl.BlockSpec((1, d), lambda b, i: (0, 0)),
                  pl.BlockSpec((None, 1, d), lambda b, i: (b, 0, 0)),
                  pl.BlockSpec((None, 1, d), lambda b, i: (b, 0, 0))],
        out_specs=pl.BlockSpec((None, tr, d), lambda b, i: (b, i, 0)),
        compiler_params=_cparams(("parallel", "parallel")),
        name="norm_mod",
    )(x, g.reshape(1, d), sc.reshape(nb, 1, d), sh.reshape(nb, 1, d))


def _norm_res_kernel(x_ref, y_ref, g_ref, gt_ref, o_ref):
    o_ref[...] = x_ref[...] + gt_ref[...] * _rms(y_ref[...].astype(F32), g_ref[...])


def _norm_res_call(x, y, g, gate):
    nb, r, d = x.shape
    tr = min(256, r)
    return pl.pallas_call(
        _norm_res_kernel,
        out_shape=jax.ShapeDtypeStruct((nb, r, d), F32),
        grid=(nb, r // tr),
        in_specs=[pl.BlockSpec((None, tr, d), lambda b, i: (b, i, 0)),
                  pl.BlockSpec((None, tr, d), lambda b, i: (b, i, 0)),
                  pl.BlockSpec((1, d), lambda b, i: (0, 0)),
                  pl.BlockSpec((None, 1, d), lambda b, i: (b, 0, 0))],
        out_specs=pl.BlockSpec((None, tr, d), lambda b, i: (b, i, 0)),
        compiler_params=_cparams(("parallel", "parallel")),
        name="norm_res",
    )(x, y.reshape(nb, r, d), g.reshape(1, d), gate.reshape(nb, 1, d))


def _norm_res_mod_kernel(x_ref, y_ref, g_ref, gt_ref, g2_ref, sc_ref, sh_ref, o_ref, h_ref):
    x = x_ref[...] + gt_ref[...] * _rms(y_ref[...].astype(F32), g_ref[...])
    o_ref[...] = x
    h_ref[...] = (_rms(x, g2_ref[...]) * (1.0 + sc_ref[...]) + sh_ref[...]).astype(h_ref.dtype)


def _norm_res_mod_call(x, y, g, gate, g_next, sc, sh):
    nb, r, d = x.shape
    tr = min(256, r)
    row = pl.BlockSpec((None, tr, d), lambda b, i: (b, i, 0))
    vec = pl.BlockSpec((None, 1, d), lambda b, i: (b, 0, 0))
    gain = pl.BlockSpec((1, d), lambda b, i: (0, 0))
    return pl.pallas_call(
        _norm_res_mod_kernel,
        out_shape=(jax.ShapeDtypeStruct((nb, r, d), F32), jax.ShapeDtypeStruct((nb, r, d), BF16)),
        grid=(nb, r // tr),
        in_specs=[row, row, gain, vec, gain, vec, vec],
        out_specs=(row, row),
        compiler_params=_cparams(("parallel", "parallel")),
        name="norm_res_mod",
    )(x, y.reshape(nb, r, d), g.reshape(1, d), gate.reshape(nb, 1, d), g_next.reshape(1, d),
      sc.reshape(nb, 1, d), sh.reshape(nb, 1, d))


def _matmul_kernel(a_ref, w_ref, o_ref, acc_ref, *, nk):
    k = pl.program_id(2)
    part = jnp.dot(a_ref[...], w_ref[...], preferred_element_type=F32)
    if nk == 1:
        o_ref[...] = part.astype(o_ref.dtype)
    else:
        @pl.when(k == 0)
        def _():
            acc_ref[...] = part

        @pl.when(jnp.logical_and(k > 0, k < nk - 1))
        def _():
            acc_ref[...] += part

        @pl.when(k == nk - 1)
        def _():
            o_ref[...] = (acc_ref[...] + part).astype(o_ref.dtype)


def _pick(n, prefs):
    for t in prefs:
        if n % t == 0:
            return t
    return n


def _matmul_call(a, w, out_dtype):
    m, kdim = a.shape
    _, n = w.shape
    tm = _pick(m, (1024, 512, 256) if kdim <= 4096 else (512, 256))
    tn = _pick(n, (512, 256, 128))
    operand_bytes = kdim * (tm + tn) * 2 * 2
    tk = kdim if operand_bytes <= VMEM_LIMIT * 6 // 7 else _pick(kdim, (5504, 4096, 2048))
    nk = kdim // tk
    return pl.pallas_call(
        functools.partial(_matmul_kernel, nk=nk),
        out_shape=jax.ShapeDtypeStruct((m, n), out_dtype),
        grid=(m // tm, n // tn, nk),
        in_specs=[pl.BlockSpec((tm, tk), lambda i, j, k: (i, k)),
                  pl.BlockSpec((tk, tn), lambda i, j, k: (k, j))],
        out_specs=pl.BlockSpec((tm, tn), lambda i, j, k: (i, j)),
        scratch_shapes=[pltpu.VMEM((tm, tn) if nk > 1 else (8, LANE), F32)],
        compiler_params=_cparams(("parallel", "parallel", "arbitrary")),
        name="matmul",
    )(a, w)


def _matmul_ws_kernel(a_ref, w_ref, o_ref, wb_ref):
    @pl.when(pl.program_id(1) == 0)
    def _():
        wb_ref[...] = w_ref[...].astype(BF16)

    o_ref[...] = jnp.dot(a_ref[...], wb_ref[...], preferred_element_type=F32).astype(o_ref.dtype)


def _matmul_ws_call(a, w, layer, out_dtype, col0=0):
    m, kdim = a.shape
    n = w.shape[2] - col0
    tm = _pick(m, (1024, 512, 256))
    tn = _pick(n, (512, 256, 128))
    assert col0 % tn == 0
    cb0 = col0 // tn
    return pl.pallas_call(
        _matmul_ws_kernel,
        out_shape=jax.ShapeDtypeStruct((m, n), out_dtype),
        grid=(n // tn, m // tm),
        in_specs=[pl.BlockSpec((tm, kdim), lambda j, i: (i, 0)),
                  pl.BlockSpec((None, kdim, tn), lambda j, i: (layer, 0, cb0 + j))],
        out_specs=pl.BlockSpec((tm, tn), lambda j, i: (i, j)),
        scratch_shapes=[pltpu.VMEM((kdim, tn), BF16)],
        compiler_params=_cparams(("parallel", "arbitrary")),
        name="matmul_ws",
    )(a, w)


def _swap_pairs(y):
    lane = lax.broadcasted_iota(jnp.int32, y.shape, 1)
    nxt = pltpu.roll(y, LANE - 1, 1)
    prv = pltpu.roll(y, 1, 1)
    return jnp.where((lane & 1) == 0, nxt, prv)


def _prep_tile(y, gain, cos, sin, scale):
    y = y.astype(F32)
    if gain is not None:
        y = _rms(y, gain)
    if cos is not None:
        y = y * cos + _swap_pairs(y) * sin
    if scale != 1.0:
        y = y * scale
    return y.astype(BF16)


def _prep_kernel(p_ref, cos_ref, sin_ref, g_ref, o_ref, *, n_chunks, norm, rope, scale):
    for c in range(n_chunks):
        sl = slice(c * LANE, (c + 1) * LANE)
        o_ref[:, sl] = _prep_tile(p_ref[:, sl], g_ref[...] if norm else None,
                                  cos_ref[...] if rope else None,
                                  sin_ref[...] if rope else None, scale)


def _prep_call(p, col0, width, seq, *, gain=None, rope=None, scale=1.0):
    rows = p.shape[0]
    tr = min(256, seq)
    n_seq_tiles = seq // tr
    bw = min(width, 512)
    assert (col0 * LANE) % bw == 0 and width % bw == 0
    cb0 = col0 * LANE // bw
    if rope is None:
        cos = sin = jnp.zeros((tr, LANE), F32)
        tab_map = lambda i, j: (0, 0)
    else:
        cos, sin = rope
        tab_map = lambda i, j: (i % n_seq_tiles, 0)
    g = (jnp.ones((HEAD_DIM,), F32) if gain is None else gain).reshape(1, HEAD_DIM)
    kern = functools.partial(_prep_kernel, n_chunks=bw // LANE, norm=gain is not None,
                             rope=rope is not None, scale=float(scale))
    return pl.pallas_call(
        kern,
        out_shape=jax.ShapeDtypeStruct((rows, width), BF16),
        grid=(rows // tr, width // bw),
        in_specs=[pl.BlockSpec((tr, bw), lambda i, j: (i, cb0 + j)),
                  pl.BlockSpec((tr, LANE), tab_map),
                  pl.BlockSpec((tr, LANE), tab_map),
                  pl.BlockSpec((1, HEAD_DIM), lambda i, j: (0, 0))],
        out_specs=pl.BlockSpec((tr, bw), lambda i, j: (i, j)),
        compiler_params=_cparams(("parallel", "parallel")),
        name="qk_prep",
    )(p, cos, sin, g)


def _rope_tables(seq, head_dim):
    t = np.arange(seq)
    rows = (t // GRID_W).astype(np.float32)
    cols = (t % GRID_W).astype(np.float32)
    quarter = head_dim // 4
    inv = (ROPE_THETA ** (-np.arange(quarter, dtype=np.float32) / quarter)).astype(np.float32)
    ang = np.concatenate([rows[:, None] * inv, cols[:, None] * inv], axis=-1)
    lane = np.arange(LANE)
    pair = (lane % head_dim) // 2
    sign = np.where(lane % 2 == 0, -1.0, 1.0).astype(np.float32)
    cos = np.cos(ang).astype(np.float32)[:, pair]
    sin = np.sin(ang).astype(np.float32)[:, pair] * sign
    return jnp.asarray(cos), jnp.asarray(sin)


def _qk(q, k):
    return lax.dot_general(q, k, (((1,), (1,)), ((), ())), preferred_element_type=F32)


def _stack_heads(q, n):
    if n == 1:
        return q
    return jnp.concatenate([q[:, g * LANE:(g + 1) * LANE] for g in range(n)], axis=0)


def _unstack_heads(o, n):
    if n == 1:
        return o
    t = o.shape[0] // n
    return jnp.concatenate([o[g * t:(g + 1) * t] for g in range(n)], axis=1)


def _stack_diff(q):
    lane = lax.broadcasted_iota(jnp.int32, q.shape, 1)
    zero = jnp.zeros_like(q)
    return jnp.concatenate([jnp.where(lane < DH_B, q, zero),
                            jnp.where(lane >= DH_B, q, zero)], axis=0)


def _softmax_attend(groups, sink=None):
    m = None
    for s, _ in groups:
        sm = jnp.max(s, axis=-1, keepdims=True)
        m = sm if m is None else jnp.maximum(m, sm)
    if sink is not None:
        m = jnp.maximum(m, sink)
    l = None
    acc = None
    for s, v in groups:
        p = jnp.exp(s - m)
        ps = jnp.sum(p, axis=-1, keepdims=True)
        pv = jnp.dot(p.astype(v.dtype), v, preferred_element_type=F32)
        l = ps if l is None else l + ps
        acc = pv if acc is None else acc + pv
    if sink is not None:
        l = l + jnp.exp(sink - m)
    return acc / l


def _diff_lambda(lam_ref, lam_init):
    a = jnp.sum(lam_ref[0:1, :] * lam_ref[1:2, :], axis=-1, keepdims=True)
    b = jnp.sum(lam_ref[2:3, :] * lam_ref[3:4, :], axis=-1, keepdims=True)
    return jnp.exp(a) - jnp.exp(b) + lam_init


def _diff_finalize(o, lam_ref, gsub_ref, lam_init):
    t = o.shape[0] // 2
    d = o[:t] - _diff_lambda(lam_ref, lam_init) * o[t:]
    return _rms(d, gsub_ref[...]) * (1.0 - lam_init)


def _sink_rows(sink_ref, kv, rows_per_head, n_rows):
    head = lax.broadcasted_iota(jnp.int32, (n_rows, 1), 0) // rows_per_head
    out = jnp.zeros((n_rows, 1), F32)
    for g in range(GQA_G):
        out = jnp.where(head == g, sink_ref[kv * GQA_G + g], out)
    return out


def _lane_tiles(s):
    return [s[:, j * LANE:(j + 1) * LANE] for j in range(s.shape[1] // LANE)]


def _flash_kernel(q_ref, kl_ref, vl_ref, kc_ref, vc_ref, cq_ref, sq_ref, ck_ref, sk_ref,
                  gq_ref, gk_ref, lam_ref, gsub_ref, o_ref, kp_ref, kcp_ref, *,
                  diff, n_stack, tk, lam_init, norm, q_scale, q_ready):
    gq = gq_ref[...] if norm else None
    gk = gk_ref[...] if norm else None

    @pl.when(pl.program_id(2) == 0)
    def _():
        for c in range(kl_ref.shape[0] // tk):
            sl = slice(c * tk, (c + 1) * tk)
            kp_ref[sl, :] = _prep_tile(kl_ref[sl, :], gk, ck_ref[sl, :], sk_ref[sl, :], 1.0)
        kcp_ref[...] = _prep_tile(kc_ref[...], gk, None, None, 1.0)

    kl_ref, kc_ref = kp_ref, kcp_ref
    if q_ready:
        q = q_ref[...]
    else:
        cq, sq = cq_ref[...], sq_ref[...]
        q = jnp.concatenate([_prep_tile(q_ref[:, g * LANE:(g + 1) * LANE], gq, cq, sq, q_scale)
                             for g in range(q_ref.shape[1] // LANE)], axis=1)
    q = _stack_diff(q) if diff else _stack_heads(q, n_stack)

    def row_max(tiles):
        m = tiles[0]
        for t in tiles[1:]:
            m = jnp.maximum(m, t)
        return jnp.broadcast_to(jnp.max(m, axis=-1, keepdims=True), m.shape)

    def weights(tiles, m, v):
        ps = [jnp.exp2(t - m) for t in tiles]
        l = ps[0]
        for p in ps[1:]:
            l = l + p
        pv = jnp.dot(jnp.concatenate(ps, axis=1).astype(v.dtype), v, preferred_element_type=F32)
        return l, pv

    def latent_scores(c):
        return _lane_tiles(_qk(q, kl_ref[c * tk:(c + 1) * tk, :]))

    n_chunks = kl_ref.shape[0] // tk
    tiles = _lane_tiles(_qk(q, kc_ref[...]))
    nxt = latent_scores(0)
    m = row_max(tiles)
    l, acc = weights(tiles, m, vc_ref[...])
    for c in range(n_chunks):
        tiles = nxt
        if c + 1 < n_chunks:
            nxt = latent_scores(c + 1)
        m_new = jnp.maximum(m, row_max(tiles))
        alpha = jnp.exp2(m - m_new)
        l_c, pv = weights(tiles, m_new, vl_ref[c * tk:(c + 1) * tk, :])
        l = alpha * l + l_c
        acc = alpha * acc + pv
        m = m_new
    o = acc / jnp.sum(l, axis=-1, keepdims=True)
    if diff:
        o = _diff_finalize(o, lam_ref, gsub_ref, lam_init)
    else:
        o = _unstack_heads(o, n_stack)
    o_ref[...] = o.astype(o_ref.dtype)


def _flash_call(p, pc, pc_off, q_col, k_col, v_col, rope, q_scale, *, batch, seq, ctx, diff,
                q_ready=None, g_q=None, g_k=None, lam_rows=None, g_sub=None, lam_init=0.0):
    n_heads = N_GROUP_HEADS if diff else KV_GQA
    n_stack = 2 if diff else GQA_G
    tq = min(1024 // n_stack, seq)
    qw = LANE if diff else GQA_G * LANE
    qb = 0 if q_ready is not None else q_col * LANE // qw
    nq = seq // tq
    norm = g_k is not None
    ones = jnp.ones((HEAD_DIM,), F32)
    if lam_rows is None:
        lam_rows = jnp.zeros((8, LANE), F32)
    vec = lambda g: (ones if g is None else g).reshape(1, HEAD_DIM)
    cos, sin = rope
    kern = functools.partial(_flash_kernel, diff=diff, n_stack=n_stack, tk=min(1024, seq),
                             lam_init=float(lam_init), norm=norm, q_scale=float(q_scale),
                             q_ready=q_ready is not None)
    const = lambda b, h, i: (0, 0)
    return pl.pallas_call(
        kern,
        out_shape=jax.ShapeDtypeStruct((batch * seq, GROUP_W), BF16),
        grid=(batch, n_heads, nq),
        in_specs=[pl.BlockSpec((tq, qw), lambda b, h, i: (b * nq + i, qb + h)),
                  pl.BlockSpec((seq, LANE), lambda b, h, i: (b, k_col + h)),
                  pl.BlockSpec((seq, LANE), lambda b, h, i: (b, v_col + h)),
                  pl.BlockSpec((ctx, LANE), lambda b, h, i: (b, k_col - pc_off + h)),
                  pl.BlockSpec((ctx, LANE), lambda b, h, i: (b, v_col - pc_off + h)),
                  pl.BlockSpec((tq, LANE), lambda b, h, i: (i, 0)),
                  pl.BlockSpec((tq, LANE), lambda b, h, i: (i, 0)),
                  pl.BlockSpec((seq, LANE), const),
                  pl.BlockSpec((seq, LANE), const),
                  pl.BlockSpec((1, HEAD_DIM), const),
                  pl.BlockSpec((1, HEAD_DIM), const),
                  pl.BlockSpec((8, LANE), const),
                  pl.BlockSpec((1, HEAD_DIM), const)],
        out_specs=pl.BlockSpec((tq, qw), lambda b, h, i: (b * nq + i, h)),
        scratch_shapes=[pltpu.VMEM((seq, LANE), BF16), pltpu.VMEM((ctx, LANE), BF16)],
        compiler_params=_cparams(("parallel", "parallel", "arbitrary")),
        name="attn_full",
    )(p if q_ready is None else q_ready, p, p, pc, pc, cos, sin, cos, sin, vec(g_q), vec(g_k),
      lam_rows, vec(g_sub))


def _band_kernel(sink_ref, q_ref, kp_ref, kx_ref, kn_ref, vp_ref, vx_ref, vn_ref,
                 kc_ref, vc_ref, o_ref, *, tq, seq):
    kv = pl.program_id(1)
    i = pl.program_id(2)
    q = _stack_heads(q_ref[...], GQA_G)
    rows = q.shape[0]
    k_loc = jnp.concatenate([kp_ref[...], kx_ref[...], kn_ref[...]], axis=0)
    v_loc = jnp.concatenate([vp_ref[...], vx_ref[...], vn_ref[...]], axis=0)
    s_loc = _qk(q, k_loc)
    qpos = i * tq + (lax.broadcasted_iota(jnp.int32, s_loc.shape, 0) & (tq - 1))
    kpos = (i - 1) * tq + lax.broadcasted_iota(jnp.int32, s_loc.shape, 1)
    ok = jnp.where(kpos >= 0, jnp.abs(qpos - kpos), WIN_C + 1)
    ok = jnp.where(kpos < seq, ok, WIN_C + 1) <= WIN_C
    s_loc = jnp.where(ok, s_loc, NEG_INF)
    s_ctx = _qk(q, kc_ref[...])
    sink = _sink_rows(sink_ref, kv, tq, rows)
    o = _softmax_attend([(s_loc, v_loc), (s_ctx, vc_ref[...])], sink)
    o_ref[...] = _unstack_heads(o, GQA_G).astype(o_ref.dtype)


def _band_call(sink, q, kl, p, pc, pc_off, *, batch, seq, ctx):
    tq = WIN_C
    nq = seq // tq
    qw = GQA_G * LANE

    def nb(delta, col):
        def imap(b, h, i):
            return (b * nq + jnp.clip(i + delta, 0, nq - 1), col + h)
        return imap

    return pl.pallas_call(
        functools.partial(_band_kernel, tq=tq, seq=seq),
        out_shape=jax.ShapeDtypeStruct((batch * seq, GROUP_W), BF16),
        grid=(batch, KV_GQA, nq),
        in_specs=[pl.BlockSpec(memory_space=pltpu.SMEM),
                  pl.BlockSpec((tq, qw), lambda b, h, i: (b * nq + i, h)),
                  pl.BlockSpec((tq, LANE), nb(-1, 0)),
                  pl.BlockSpec((tq, LANE), nb(0, 0)),
                  pl.BlockSpec((tq, LANE), nb(1, 0)),
                  pl.BlockSpec((tq, LANE), nb(-1, VC_COL)),
                  pl.BlockSpec((tq, LANE), nb(0, VC_COL)),
                  pl.BlockSpec((tq, LANE), nb(1, VC_COL)),
                  pl.BlockSpec((ctx, LANE), lambda b, h, i: (b, KC_COL - pc_off + h)),
                  pl.BlockSpec((ctx, LANE), lambda b, h, i: (b, VC_COL - pc_off + h))],
        out_specs=pl.BlockSpec((tq, qw), lambda b, h, i: (b * nq + i, h)),
        compiler_params=_cparams(("parallel", "parallel", "parallel")),
        name="attn_band",
    )(sink, q, kl, kl, kl, p, p, p, pc, pc)


def _nbr_build_bias(tab_ref, bias_ref):
    shape = (GRID_W, LANE)
    w = lax.broadcasted_iota(jnp.int32, shape, 0)
    lane = lax.broadcasted_iota(jnp.int32, shape, 1)
    c = lane & (GRID_W - 1)
    cs = jnp.clip(w - NA_KW // 2, 0, GRID_W - NA_KW)
    d = c - cs
    mask = jnp.where(d >= 0, jnp.where(d < NA_KW, 0.0, NEG_INF), NEG_INF).astype(F32)
    neg = jnp.full(shape, NEG_INF, F32)
    toep = [pltpu.roll(jnp.broadcast_to(tab_ref[ro:ro + 1, :], shape), LANE - (NA_KW - 1), 1,
                       stride=1, stride_axis=0) + mask for ro in range(2 * NA_KH - 1)]
    kinds = (lambda dr: (dr, max(dr - NA_KH // 2, 0)),
             lambda dr: (NA_KH // 2 + dr, dr),
             lambda dr: (NBR_GROUP + dr, min(NA_KH // 2 + dr, NBR_SLAB - NA_KH)))
    for kind, rel in enumerate(kinds):
        for dr in range(NBR_GROUP):
            r_rel, rs_rel = rel(dr)
            for j in range(NBR_SLAB // 2):
                halves = [toep[kr - r_rel + NA_KH - 1] if rs_rel <= kr < rs_rel + NA_KH else neg
                          for kr in (2 * j, 2 * j + 1)]
                tile = neg if halves[0] is neg and halves[1] is neg else jnp.where(
                    lane < GRID_W, halves[0], halves[1])
                bias_ref[kind, dr * GRID_W:(dr + 1) * GRID_W, j * LANE:(j + 1) * LANE] = tile


def _nbr_kernel(q_ref, k_ref, v_ref, kc_ref, vc_ref, tab_ref, o_ref, bias_ref, *, n_rows, scale):
    n_groups = n_rows // NBR_GROUP
    gq = NBR_GROUP * GRID_W
    slab = NBR_SLAB * GRID_W
    kc = kc_ref[...]
    vc = vc_ref[...]
    _nbr_build_bias(tab_ref, bias_ref)

    def keys_of(g):
        base = min(max(g * NBR_GROUP - NA_KH // 2, 0), n_rows - NBR_SLAB)
        return slice(base * GRID_W, base * GRID_W + slab)

    def scores(g):
        q = q_ref[g * gq:(g + 1) * gq, :]
        return _qk(q, k_ref[keys_of(g), :]), _qk(q, kc)

    nxt = scores(0)
    for g in range(n_groups):
        raw_loc, raw_ctx = nxt
        if g + 1 < n_groups:
            nxt = scores(g + 1)
        kind = 0 if g == 0 else (2 if g == n_groups - 1 else 1)
        s_loc = raw_loc * scale + bias_ref[kind]
        o = _softmax_attend([(s_loc, v_ref[keys_of(g), :]), (raw_ctx * scale, vc)])
        o_ref[g * gq:(g + 1) * gq, :] = o.astype(o_ref.dtype)


def _nbr_table(na_bias):
    h, nr, nc = na_bias.shape
    t = jnp.pad(na_bias.astype(F32), ((0, 0), (0, 16 - nr), (0, GRID_W - nc)))
    return jnp.concatenate([t] * (LANE // GRID_W), axis=-1)


def _nbr_call(p, pc, pc_off, na_bias, *, batch, seq, ctx):
    n_rows = seq // GRID_W
    assert n_rows >= NBR_SLAB and n_rows % NBR_GROUP == 0 and 2 * GRID_W == LANE
    assert NBR_SLAB >= NBR_GROUP + NA_KH - 1 and NBR_SLAB == NBR_GROUP + NA_KH
    scale = HEAD_DIM ** -0.5
    return pl.pallas_call(
        functools.partial(_nbr_kernel, n_rows=n_rows, scale=scale),
        out_shape=jax.ShapeDtypeStruct((batch * seq, GROUP_W), BF16),
        grid=(batch, N_GROUP_HEADS),
        in_specs=[pl.BlockSpec((seq, LANE), lambda b, h: (b, QD_COL + h)),
                  pl.BlockSpec((seq, LANE), lambda b, h: (b, KD_COL + h)),
                  pl.BlockSpec((seq, LANE), lambda b, h: (b, VD_COL + h)),
                  pl.BlockSpec((ctx, LANE), lambda b, h: (b, KD_COL - pc_off + h)),
                  pl.BlockSpec((ctx, LANE), lambda b, h: (b, VD_COL - pc_off + h)),
                  pl.BlockSpec((None, 16, LANE), lambda b, h: (h, 0, 0))],
        out_specs=pl.BlockSpec((seq, LANE), lambda b, h: (b, h)),
        scratch_shapes=[pltpu.VMEM((3, NBR_GROUP * GRID_W, NBR_SLAB * GRID_W), F32)],
        compiler_params=_cparams(("parallel", "parallel")),
        name="attn_nbr",
    )(p, p, p, pc, pc, _nbr_table(na_bias))


def _ctx_kernel(sink_ref, q_ref, k_ref, v_ref, lam_ref, gsub_ref, o_ref, *,
                diff, n_stack, scale, use_sink, lam_init):
    q = q_ref[...]
    q = _stack_diff(q) if diff else _stack_heads(q, n_stack)
    s = _qk(q, k_ref[...])
    if scale != 1.0:
        s = s * scale
    sink = None
    if use_sink:
        sink = _sink_rows(sink_ref, pl.program_id(1), q_ref.shape[0], q.shape[0])
    o = _softmax_attend([(s, v_ref[...])], sink)
    if diff:
        o = _diff_finalize(o, lam_ref, gsub_ref, lam_init)
    else:
        o = _unstack_heads(o, n_stack)
    o_ref[...] = o.astype(o_ref.dtype)


def _ctx_call(q, q_col, k, k_col, v, v_col, *, batch, ctx, n_heads, n_stack, diff=False,
              scale=1.0, sink=None, lam_rows=None, g_sub=None, lam_init=0.0):
    qw = LANE if diff else n_stack * LANE
    qb = q_col * LANE // qw
    if lam_rows is None:
        lam_rows = jnp.zeros((8, LANE), F32)
        g_sub = jnp.ones((HEAD_DIM,), F32)
    use_sink = sink is not None
    if sink is None:
        sink = jnp.zeros((N_GROUP_HEADS,), F32)
    kern = functools.partial(_ctx_kernel, diff=diff, n_stack=n_stack, scale=float(scale),
                             use_sink=use_sink, lam_init=float(lam_init))
    return pl.pallas_call(
        kern,
        out_shape=jax.ShapeDtypeStruct((batch * ctx, GROUP_W), BF16),
        grid=(batch, n_heads),
        in_specs=[pl.BlockSpec(memory_space=pltpu.SMEM),
                  pl.BlockSpec((ctx, qw), lambda b, h: (b, qb + h)),
                  pl.BlockSpec((ctx, LANE), lambda b, h: (b, k_col + h)),
                  pl.BlockSpec((ctx, LANE), lambda b, h: (b, v_col + h)),
                  pl.BlockSpec((8, LANE), lambda b, h: (0, 0)),
                  pl.BlockSpec((1, HEAD_DIM), lambda b, h: (0, 0))],
        out_specs=pl.BlockSpec((ctx, qw), lambda b, h: (b, h)),
        compiler_params=_cparams(("parallel", "parallel")),
        name="attn_ctx",
    )(sink, q, k, v, lam_rows, g_sub.reshape(1, HEAD_DIM))


def _ffn_up_kernel(a_ref, ap_ref, an_ref, wg_ref, wv_ref, cwg_ref, cwv_ref, *rest,
                   tiles_per_seq, seq_rows, cast_down):
    if cast_down:
        wd_ref, o_ref, wdb_ref, ext_ref = rest
        wdb_ref[...] = wd_ref[...].astype(BF16)
    else:
        o_ref, ext_ref = rest
    i = pl.program_id(0)
    tm = a_ref.shape[0]

    @pl.when(pl.program_id(1) == 0)
    def _():
        first = (i % tiles_per_seq) == 0
        last = (i % tiles_per_seq) == tiles_per_seq - 1
        zero = jnp.zeros(ap_ref.shape, ap_ref.dtype)
        ext_ref[0:HALO, :] = jnp.where(first, zero, ap_ref[...])
        ext_ref[HALO:HALO + tm, :] = a_ref[...]
        ext_ref[HALO + tm:, :] = jnp.where(last, zero, an_ref[...])

    a = ext_ref[...]
    rows = min(EPI_ROWS, tm)

    def conv(u, r0, cw_ref):
        w = u[HALO + r0 - 8:HALO + r0 + rows + 8]
        n = w.shape[0]
        below, above = pltpu.roll(w, 1, 0), pltpu.roll(w, n - 1, 0)
        starts = [b for b in range(seq_rows, tm, seq_rows) if r0 - 8 <= b <= r0 + rows + 8]
        if starts:
            row = lax.broadcasted_iota(jnp.int32, w.shape, 0) + (r0 - 8)
            for b in starts:
                below = jnp.where(row == b, 0.0, below)
                above = jnp.where(row == b - 1, 0.0, above)
        cu = below * cw_ref[0:1, :] + w * cw_ref[1:2, :] + above * cw_ref[2:3, :]
        return cu[8:8 + rows]

    ug = jnp.dot(a, wg_ref[...].astype(BF16), preferred_element_type=F32)
    uv = jnp.dot(a, wv_ref[...].astype(BF16), preferred_element_type=F32)
    for r0 in range(0, tm, rows):
        gate = conv(ug, r0, cwg_ref)
        val = conv(uv, r0, cwv_ref)
        o_ref[r0:r0 + rows, :] = (gate * jax.nn.sigmoid(gate) * val).astype(o_ref.dtype)


def _ffn_up_call(a, w_up, layer, conv_w, seq, w_down=None):
    m, kdim = a.shape
    f = w_up.shape[2] // 2
    tm = min(1024, seq)
    if seq < 1024 and min(1024, m) % seq == 0 and m % min(1024, m) == 0:
        tm = min(1024, m)
    seq_rows = min(seq, tm)
    tn = _pick(f, (256, 128))
    tiles_per_seq = max(seq // tm, 1)
    fb = f // tn
    hb = tm // HALO
    n_halo = m // HALO
    n_steps = (m // tm) * fb
    cw = jnp.pad(conv_w, ((0, 8 - CONV_W), (0, 0)))
    in_specs = [pl.BlockSpec((tm, kdim), lambda i, j: (i, 0)),
                pl.BlockSpec((HALO, kdim), lambda i, j: (jnp.maximum(i * hb - 1, 0), 0)),
                pl.BlockSpec((HALO, kdim), lambda i, j: (jnp.minimum((i + 1) * hb, n_halo - 1), 0)),
                pl.BlockSpec((None, kdim, tn), lambda i, j: (layer, 0, j)),
                pl.BlockSpec((None, kdim, tn), lambda i, j: (layer, 0, fb + j)),
                pl.BlockSpec((8, tn), lambda i, j: (0, j)),
                pl.BlockSpec((8, tn), lambda i, j: (0, fb + j))]
    out_shape = jax.ShapeDtypeStruct((m, f), BF16)
    out_specs = pl.BlockSpec((tm, tn), lambda i, j: (i, j))
    args = [a, a, a, w_up, w_up, cw, cw]
    cast_down = w_down is not None
    if cast_down:
        fd, dd = w_down.shape[1:]
        slab = fd // n_steps
        assert fd % n_steps == 0 and slab % HALO == 0
        in_specs.append(pl.BlockSpec((None, slab, dd), lambda i, j: (layer, i * fb + j, 0)))
        out_shape = (out_shape, jax.ShapeDtypeStruct((fd, dd), BF16))
        out_specs = (out_specs, pl.BlockSpec((slab, dd), lambda i, j: (i * fb + j, 0)))
        args.append(w_down)
    return pl.pallas_call(
        functools.partial(_ffn_up_kernel, tiles_per_seq=tiles_per_seq, seq_rows=seq_rows,
                          cast_down=cast_down),
        out_shape=out_shape,
        grid=(m // tm, fb),
        in_specs=in_specs,
        out_specs=out_specs,
        scratch_shapes=[pltpu.VMEM((tm + 2 * HALO, kdim), BF16)],
        compiler_params=_cparams(("parallel", "arbitrary")),
        name="ffn_up",
    )(*args)


def kernel(x, c, ctx, c_ctx, w_ada, b_ada, g_attn_pre, g_attn_post, g_mlp_pre, g_mlp_post,
           w_in, w_out, g_q_a, g_k_a, lam_q1, lam_k1, lam_q2, lam_k2, g_sub_b, sink_c, na_bias,
           w_up, conv_ffn_w, w_down):
    batch, seq, d = x.shape
    n_ctx = ctx.shape[1]
    depth = w_ada.shape[0]
    assert batch + 1 <= 8 and d == N_Q

    cond = jnp.zeros((8, d), F32).at[:batch].set(c).at[batch].set(c_ctx)
    mod_all = _ada_call(cond, w_ada, b_ada)
    rope_hd = _rope_tables(seq, HEAD_DIM)
    rope_dh = _rope_tables(seq, DH_B)
    scale_hd = HEAD_DIM ** -0.5
    scale_dh = DH_B ** -0.5

    def latent_mod(l):
        mod = mod_all[l, :batch].reshape(batch, 6, d)
        return tuple(mod[:, k] for k in range(6))

    h = ctx
    sh_a, sc_a, gt_a, sh_m, sc_m, gt_m = latent_mod(0)
    hx = _norm_mod_call(x, g_attn_pre[0], sc_a, sh_a)
    for l in range(depth):
        last = l == depth - 1
        lam_init = 0.8 - 0.6 * math.exp(-0.3 * l)
        sh_a, sc_a, gt_a, sh_m, sc_m, gt_m = latent_mod(l)
        mod_c = mod_all[l, batch].reshape(6, d)
        one = lambda v: v.reshape(1, d)
        lam_rows = jnp.zeros((8, LANE), F32).at[:4, :DH_B].set(
            jnp.stack([lam_q1[l], lam_k1[l], lam_q2[l], lam_k2[l]]))

        hx = hx.reshape(batch * seq, d)
        hc = _norm_mod_call(h.reshape(1, batch * n_ctx, d), g_attn_pre[l],
                            one(mod_c[1]), one(mod_c[0])).reshape(batch * n_ctx, d)
        p = _matmul_ws_call(hx, w_in, l, BF16)
        pc_off = 32 if last else 0
        pc = _matmul_ws_call(hc, w_in, l, BF16, col0=pc_off * LANE)

        dims = dict(batch=batch, seq=seq, ctx=n_ctx)
        q_c = _prep_call(p, QC_COL, GROUP_W, seq, rope=rope_hd, scale=scale_hd)
        k_c = _prep_call(p, KC_COL, KV_GQA * LANE, seq, rope=rope_hd)
        q_a = _prep_call(p, QA_COL, GROUP_W, seq, gain=g_q_a[l], rope=rope_hd,
                         scale=scale_hd * LOG2E)
        o_a = _flash_call(p, pc, pc_off, QA_COL, KA_COL, VA_COL, rope_hd, 1.0, diff=False,
                          q_ready=q_a, g_k=g_k_a[l], **dims)
        o_b = _flash_call(p, pc, pc_off, QB_COL, KB_COL, VB_COL, rope_dh, scale_dh * LOG2E,
                          diff=True, lam_rows=lam_rows, g_sub=g_sub_b[l], lam_init=lam_init, **dims)
        o_c = _band_call(sink_c[l], q_c, k_c, p, pc, pc_off, **dims)
        o_d = _nbr_call(p, pc, pc_off, na_bias[l], **dims)
        o = jnp.concatenate([o_a, o_b, o_c, o_d], axis=-1)
        y = _matmul_ws_call(o, w_out, l, BF16)
        x, hx = _norm_res_mod_call(x, y, g_attn_post[l], gt_a, g_mlp_pre[l], sc_m, sh_m)

        g, wd = _ffn_up_call(hx.reshape(batch * seq, d), w_up, l, conv_ffn_w[l], seq, w_down)
        y = _matmul_call(g, wd, BF16)
        if last:
            x = _norm_res_call(x, y, g_mlp_post[l], gt_m)
        else:
            nxt = latent_mod(l + 1)
            x, hx = _norm_res_mod_call(x, y, g_mlp_post[l], gt_m, g_attn_pre[l + 1], nxt[1], nxt[0])

        if not last:
            cdims = dict(batch=batch, ctx=n_ctx)
            qc_a = _prep_call(pc, QA_COL, GROUP_W, n_ctx, gain=g_q_a[l], scale=scale_hd)
            kc_a = _prep_call(pc, KA_COL, KV_GQA * LANE, n_ctx, gain=g_k_a[l])
            co_a = _ctx_call(qc_a, 0, kc_a, 0, pc, VA_COL, n_heads=KV_GQA, n_stack=GQA_G, **cdims)
            co_b = _ctx_call(pc, QB_COL, pc, KB_COL, pc, VB_COL, n_heads=N_GROUP_HEADS,
                             n_stack=2, diff=True, scale=scale_dh, lam_rows=lam_rows,
                             g_sub=g_sub_b[l], lam_init=lam_init, **cdims)
            co_c = _ctx_call(pc, QC_COL, pc, KC_COL, pc, VC_COL, n_heads=KV_GQA, n_stack=GQA_G,
                             scale=scale_hd, sink=sink_c[l], **cdims)
            co_d = _ctx_call(pc, QD_COL, pc, KD_COL, pc, VD_COL, n_heads=N_GROUP_HEADS,
                             n_stack=1, scale=scale_hd, **cdims)
            co = jnp.concatenate([co_a, co_b, co_c, co_d], axis=-1)
            yc = _matmul_ws_call(co, w_out, l, BF16)
            h3 = h.reshape(1, batch * n_ctx, d)
            h3 = _norm_res_call(h3, yc, g_attn_post[l], one(mod_c[2]))
            hc = _norm_mod_call(h3, g_mlp_pre[l], one(mod_c[4]), one(mod_c[3]))
            gc = _ffn_up_call(hc.reshape(batch * n_ctx, d), w_up, l, conv_ffn_w[l], n_ctx)
            yc = _matmul_call(gc, wd, BF16)
            h = _norm_res_call(h3, yc, g_mlp_post[l], one(mod_c[5])).reshape(batch, n_ctx, d)
    return x
```
